```python
import math
import jax, jax.numpy as jnp
from jax import lax
import numpy as np

D_MODEL = 2048
BATCH = 2
SEQ = 16384
DEPTH = 1
DEC_BATCH = 16
DEC_SEQ = 64
PAST_LEN = 1024

CHUNK = 64
N_META = 16
W_POOL = D_MODEL // 2
W_SSM = D_MODEL // 2
POOL_SIZES = (2, 4, 8, 16)
N_POOL_GROUPS = 4
POOL_GROUP = W_POOL // N_POOL_GROUPS
POOL_HIST = max(POOL_SIZES) - 1
SSM_GROUP = 16
N_SSM_GROUPS = W_SSM // SSM_GROUP
SSM_STATE = 64
PEER_HEADS = 8
PEER_TOPK = 16
N_KEYS = 128
N_EXPERTS = N_KEYS * N_KEYS
D_QUERY = 256
D_HALF = D_QUERY // 2
PEER_BLOCK = 128
W_IN_COLS = W_POOL + W_SSM + 2 * D_MODEL
NORM_EPS = 1e-6
F32 = jnp.float32

kernel_name = "hybrid_pool_s5_peer_stream_step"


def rmsnorm(x, g):
    xf = x.astype(F32)
    y = xf * lax.rsqrt(jnp.mean(xf * xf, axis=-1, keepdims=True) + NORM_EPS)
    return (y * g.astype(F32)).astype(x.dtype)


def pool_mixer(u, hist, n_prev, pool_w, pool_scale):
    B, L, _ = u.shape
    ext = jnp.concatenate([hist.astype(u.dtype), u], axis=1).astype(F32)
    cs = jnp.cumsum(ext, axis=1)
    cs = jnp.concatenate([jnp.zeros((B, 1, W_POOL), F32), cs], axis=1)
    end = cs[:, POOL_HIST + 1:]
    avail = jnp.arange(L) + 1 + n_prev
    pooled = []
    for gi, w in enumerate(POOL_SIZES):
        sl = slice(gi * POOL_GROUP, (gi + 1) * POOL_GROUP)
        start = cs[:, POOL_HIST + 1 - w:POOL_HIST + 1 - w + L, sl]
        cnt = jnp.minimum(avail, w).astype(F32)[None, :, None]
        pooled.append((end[..., sl] - start) / cnt)
    pooled = jnp.concatenate(pooled, axis=-1) - u.astype(F32)
    pooled = pooled.reshape(B, L, N_POOL_GROUPS, POOL_GROUP).astype(u.dtype)
    mixed = jnp.einsum('blgc,gcd->blgd', pooled, pool_w).reshape(B, L, W_POOL)
    new_hist = ext[:, -POOL_HIST:].astype(hist.dtype)
    return mixed * pool_scale, new_hist


def s5_discretize(a_re, a_im, log_step, b_re, b_im):
    dt = jnp.exp(log_step.astype(F32))[:, None]
    ar, ai = a_re.astype(F32), a_im.astype(F32)
    mag = jnp.exp(ar * dt)
    abar_re = mag * jnp.cos(ai * dt)
    abar_im = mag * jnp.sin(ai * dt)
    nr, ni = abar_re - 1.0, abar_im
    den = ar * ar + ai * ai
    f_re = (nr * ar + ni * ai) / den
    f_im = (ni * ar - nr * ai) / den
    br, bi = b_re.astype(F32), b_im.astype(F32)
    bbar_re = f_re[..., None] * br - f_im[..., None] * bi
    bbar_im = f_re[..., None] * bi + f_im[..., None] * br
    return abar_re, abar_im, bbar_re, bbar_im


def _ssm_combine(e1, e2):
    a1r, a1i, b1r, b1i = e1
    a2r, a2i, b2r, b2i = e2
    return (a2r * a1r - a2i * a1i,
            a2r * a1i + a2i * a1r,
            a2r * b1r - a2i * b1i + b2r,
            a2r * b1i + a2i * b1r + b2i)


def s5_mixer(u, h0_re, h0_im, a_re, a_im, log_step, b_re, b_im, c_re, c_im, d_skip, w_glu):
    B, L, _ = u.shape
    ug = u.astype(F32).reshape(B, L, N_SSM_GROUPS, SSM_GROUP)
    abr, abi, bbr, bbi = s5_discretize(a_re, a_im, log_step, b_re, b_im)
    bu_re = jnp.einsum('blgc,gpc->blgp', ug, bbr)
    bu_im = jnp.einsum('blgc,gpc->blgp', ug, bbi)
    h0r, h0i = h0_re.astype(F32), h0_im.astype(F32)
    bu_re = bu_re.at[:, 0].add(abr * h0r - abi * h0i)
    bu_im = bu_im.at[:, 0].add(abr * h0i + abi * h0r)
    a_full_re = jnp.broadcast_to(abr, bu_re.shape)
    a_full_im = jnp.broadcast_to(abi, bu_im.shape)
    _, _, h_re, h_im = lax.associative_scan(_ssm_combine, (a_full_re, a_full_im, bu_re, bu_im), axis=1)
    y = (jnp.einsum('blgp,gcp->blgc', h_re, c_re.astype(F32))
         - jnp.einsum('blgp,gcp->blgc', h_im, c_im.astype(F32)))
    y = y.reshape(B, L, W_SSM) + d_skip.astype(F32) * u.astype(F32)
    y = jax.nn.gelu(y).astype(u.dtype)
    out = y * jax.nn.sigmoid(y @ w_glu)
    return out, h_re[:, -1].astype(h0_re.dtype), h_im[:, -1].astype(h0_im.dtype)


def peer_block(xb, w_q, key1, key2, u_tab, v_tab):
    T = xb.shape[0]
    q = (xb @ w_q).reshape(T, PEER_HEADS, 2, D_HALF).astype(F32)
    s1 = jnp.einsum('thd,nd->thn', q[:, :, 0], key1.astype(F32))
    s2 = jnp.einsum('thd,nd->thn', q[:, :, 1], key2.astype(F32))
    v1, i1 = lax.top_k(s1, PEER_TOPK)
    v2, i2 = lax.top_k(s2, PEER_TOPK)
    cand = (v1[..., :, None] + v2[..., None, :]).reshape(T, PEER_HEADS, PEER_TOPK * PEER_TOPK)
    cidx = (i1[..., :, None] * N_KEYS + i2[..., None, :]).reshape(T, PEER_HEADS, PEER_TOPK * PEER_TOPK)
    sc, pos = lax.top_k(cand, PEER_TOPK)
    eidx = jnp.take_along_axis(cidx, pos, axis=-1)
    gate = jax.nn.softmax(sc, axis=-1)
    act = jax.nn.gelu(jnp.einsum('thkd,td->thk', u_tab[eidx], xb))
    coef = (gate * act.astype(F32)).astype(xb.dtype)
    return jnp.einsum('thk,thkd->td', coef, v_tab[eidx])


def peer(h, w_q, key1, key2, u_tab, v_tab):
    B, L, D = h.shape
    n = B * L
    nb = -(-n // PEER_BLOCK)
    flat = jnp.pad(h.reshape(n, D), ((0, nb * PEER_BLOCK - n), (0, 0)))
    out = lax.map(lambda xb: peer_block(xb, w_q, key1, key2, u_tab, v_tab),
                  flat.reshape(nb, PEER_BLOCK, D))
    return out.reshape(nb * PEER_BLOCK, D)[:n].reshape(B, L, D)


def layer(x, hist, h0_re, h0_im, n_prev, norm1_g, w_in, pool_w, pool_scale, ssm_a_re, ssm_a_im,
          ssm_log_step, ssm_b_re, ssm_b_im, ssm_c_re, ssm_c_im, ssm_d, w_glu, proj_a, proj_b, w_out,
          norm2_g, peer_wq, peer_key1, peer_key2, peer_u, peer_v):
    h = rmsnorm(x, norm1_g)
    proj = h @ w_in
    u_a = proj[..., :W_POOL]
    u_b = proj[..., W_POOL:W_POOL + W_SSM]
    g_a = proj[..., W_POOL + W_SSM:W_POOL + W_SSM + D_MODEL]
    g_b = proj[..., W_POOL + W_SSM + D_MODEL:]
    out_a, new_hist = pool_mixer(u_a, hist, n_prev, pool_w, pool_scale)
    out_b, hT_re, hT_im = s5_mixer(u_b, h0_re, h0_im, ssm_a_re, ssm_a_im, ssm_log_step,
                                   ssm_b_re, ssm_b_im, ssm_c_re, ssm_c_im, ssm_d, w_glu)
    merged = jax.nn.sigmoid(g_a) * (out_a @ proj_a) + jax.nn.sigmoid(g_b) * (out_b @ proj_b)
    x = x + merged @ w_out
    x = x + peer(rmsnorm(x, norm2_g), peer_wq, peer_key1, peer_key2, peer_u, peer_v)
    return x, new_hist, hT_re, hT_im


def trunk(x, hist, s_re, s_im, n_prev, layer_params):
    hists, sres, sims = [], [], []
    for l in range(DEPTH):
        x, nh, nr, ni = layer(x, hist[l], s_re[l], s_im[l], n_prev, *[p[l] for p in layer_params])
        hists.append(nh)
        sres.append(nr)
        sims.append(ni)
    return x, jnp.stack(hists, 0), jnp.stack(sres, 0), jnp.stack(sims, 0)


def setup_inputs(seed: int = 0) -> dict:
    key = jax.random.key(seed)
    ks = jax.random.split(key, 32)
    nrm = jax.random.normal
    G, P = N_SSM_GROUPS, SSM_STATE
    a_im_base = jnp.pi * jnp.arange(P, dtype=F32)
    return {
        "x_prompt": nrm(ks[0], (BATCH, SEQ, D_MODEL), F32),
        "x_sample": nrm(ks[1], (DEC_BATCH, DEC_SEQ, D_MODEL), F32),
        "cache_pool": nrm(ks[2], (DEPTH, DEC_BATCH, POOL_HIST, W_POOL), F32),
        "state_ssm_re": 0.1 * nrm(ks[3], (DEPTH, DEC_BATCH, G, P), F32),
        "state_ssm_im": 0.1 * nrm(ks[4], (DEPTH, DEC_BATCH, G, P), F32),
        "meta_tokens": nrm(ks[5], (N_META, D_MODEL), F32),
        "norm1_g": 1.0 + 0.05 * nrm(ks[6], (DEPTH, D_MODEL), F32),
        "w_in": nrm(ks[7], (DEPTH, D_MODEL, W_IN_COLS), F32) * D_MODEL ** -0.5,
        "pool_w": nrm(ks[8], (DEPTH, N_POOL_GROUPS, POOL_GROUP, POOL_GROUP), F32) * POOL_GROUP ** -0.5,
        "pool_scale": 1.0 + 0.1 * nrm(ks[9], (DEPTH, W_POOL), F32),
        "ssm_a_re": -0.5 + 0.01 * nrm(ks[10], (DEPTH, G, P), F32),
        "ssm_a_im": a_im_base + 0.01 * nrm(ks[11], (DEPTH, G, P), F32),
        "ssm_log_step": jax.random.uniform(ks[12], (DEPTH, G), F32, math.log(1e-3), math.log(1e-1)),
        "ssm_b_re": nrm(ks[13], (DEPTH, G, P, SSM_GROUP), F32) * (2 * SSM_GROUP) ** -0.5,
        "ssm_b_im": nrm(ks[14], (DEPTH, G, P, SSM_GROUP), F32) * (2 * SSM_GROUP) ** -0.5,
        "ssm_c_re": nrm(ks[15], (DEPTH, G, SSM_GROUP, P), F32) * P ** -0.5,
        "ssm_c_im": nrm(ks[16], (DEPTH, G, SSM_GROUP, P), F32) * P ** -0.5,
        "ssm_d": nrm(ks[17], (DEPTH, W_SSM), F32),
        "w_glu": nrm(ks[18], (DEPTH, W_SSM, W_SSM), F32) * W_SSM ** -0.5,
        "proj_a": nrm(ks[19], (DEPTH, W_POOL, D_MODEL), F32) * W_POOL ** -0.5,
        "proj_b": nrm(ks[20], (DEPTH, W_SSM, D_MODEL), F32) * W_SSM ** -0.5,
        "w_out": nrm(ks[21], (DEPTH, D_MODEL, D_MODEL), F32) * D_MODEL ** -0.5,
        "norm2_g": 1.0 + 0.05 * nrm(ks[22], (DEPTH, D_MODEL), F32),
        "peer_wq": nrm(ks[23], (DEPTH, D_MODEL, PEER_HEADS * D_QUERY), F32) * D_MODEL ** -0.5,
        "peer_key1": nrm(ks[24], (DEPTH, N_KEYS, D_HALF), F32) * D_HALF ** -0.5,
        "peer_key2": nrm(ks[25], (DEPTH, N_KEYS, D_HALF), F32) * D_HALF ** -0.5,
        "peer_u": nrm(ks[26], (DEPTH, N_EXPERTS, D_MODEL), F32) * D_MODEL ** -0.5,
        "peer_v": 0.5 * nrm(ks[27], (DEPTH, N_EXPERTS, D_MODEL), F32),
        "final_norm_g": 1.0 + 0.05 * nrm(ks[28], (D_MODEL,), F32),
    }


def reference(x_prompt, x_sample, cache_pool, state_ssm_re, state_ssm_im, meta_tokens, norm1_g, w_in,
              pool_w, pool_scale, ssm_a_re, ssm_a_im, ssm_log_step, ssm_b_re, ssm_b_im, ssm_c_re,
              ssm_c_im, ssm_d, w_glu, proj_a, proj_b, w_out, norm2_g, peer_wq, peer_key1, peer_key2,
              peer_u, peer_v, final_norm_g):
    layer_params = (norm1_g, w_in, pool_w, pool_scale, ssm_a_re, ssm_a_im, ssm_log_step, ssm_b_re,
                    ssm_b_im, ssm_c_re, ssm_c_im, ssm_d, w_glu, proj_a, proj_b, w_out, norm2_g,
                    peer_wq, peer_key1, peer_key2, peer_u, peer_v)
    bp = x_prompt.shape[0]
    meta = jnp.broadcast_to(meta_tokens.astype(x_prompt.dtype)[None], (bp, N_META, D_MODEL))
    xp = jnp.concatenate([meta, x_prompt], axis=1)
    hist0 = jnp.zeros((DEPTH, bp, POOL_HIST, W_POOL), cache_pool.dtype)
    s0_re = jnp.zeros((DEPTH, bp, N_SSM_GROUPS, SSM_STATE), state_ssm_re.dtype)
    s0_im = jnp.zeros((DEPTH, bp, N_SSM_GROUPS, SSM_STATE), state_ssm_im.dtype)
    hp, new_pool_prompt, new_ssm_re_prompt, new_ssm_im_prompt = trunk(xp, hist0, s0_re, s0_im, 0, layer_params)
    y_prompt = rmsnorm(hp, final_norm_g)[:, N_META:]
    hs, new_pool_sample, new_ssm_re_sample, new_ssm_im_sample = trunk(
        x_sample, cache_pool, state_ssm_re, state_ssm_im, PAST_LEN, layer_params)
    y_sample = rmsnorm(hs, final_norm_g)
    return (y_prompt, y_sample, new_pool_prompt, new_ssm_re_prompt, new_ssm_im_prompt,
            new_pool_sample, new_ssm_re_sample, new_ssm_im_sample)
```

```python
import functools

import jax
import jax.numpy as jnp
from jax import lax
from jax.experimental import pallas as pl
from jax.experimental.pallas import tpu as pltpu

D_MODEL = 2048
N_META = 16
W_POOL = 1024
W_SSM = 1024
POOL_SIZES = (2, 4, 8, 16)
POOL_GROUP = 256
POOL_HIST = 15
N_SSM_GROUPS = 64
SSM_GROUP = 16
SSM_STATE = 64
PEER_HEADS = 8
PEER_TOPK = 16
N_KEYS = 128
N_EXPERTS = N_KEYS * N_KEYS
D_QUERY = 256
D_HALF = 128
W_IN_COLS = W_POOL + W_SSM + 2 * D_MODEL
NORM_EPS = 1e-6

F32 = jnp.float32
BF16 = jnp.bfloat16

SSM_SB = 4
SSM_SB_CH = W_SSM // SSM_SB
SSM_SB_LANES = (N_SSM_GROUPS // SSM_SB) * SSM_STATE
SCAN_ROWS = 8
SCAN_LANES = 256

VMEM_LIMIT = 56 * 1024 * 1024


def _const_spec(shape):
    nd = len(shape)
    return pl.BlockSpec(shape, lambda *_: (0,) * nd, pipeline_mode=pl.Buffered(1))


def _params(*sem):
    return pltpu.CompilerParams(dimension_semantics=sem, vmem_limit_bytes=VMEM_LIMIT)


def _rms(x, g):
    return x * lax.rsqrt(jnp.mean(x * x, axis=-1, keepdims=True) + NORM_EPS) * g


def _inproj_kernel(x_ref, g_ref, w_ref, o_ref, xn_ref):
    @pl.when(pl.program_id(1) == 0)
    def _():
        xn_ref[...] = _rms(x_ref[...], g_ref[...]).astype(BF16)

    o_ref[...] = jnp.dot(xn_ref[...], w_ref[...], preferred_element_type=F32)


def _inproj(x, g, w_bf16, tm):
    n = x.shape[0]
    cb = 2048
    return pl.pallas_call(
        _inproj_kernel,
        out_shape=jax.ShapeDtypeStruct((n, W_IN_COLS), F32),
        grid=(n // tm, W_IN_COLS // cb),
        in_specs=[
            pl.BlockSpec((tm, D_MODEL), lambda i, j: (i, 0)),
            _const_spec((1, D_MODEL)),
            pl.BlockSpec((D_MODEL, cb), lambda i, j: (0, j)),
        ],
        out_specs=pl.BlockSpec((tm, cb), lambda i, j: (i, j)),
        scratch_shapes=[pltpu.VMEM((tm, D_MODEL), BF16)],
        compiler_params=_params("parallel", "arbitrary"),
        name="inproj",
    )(x, g, w_bf16)


def _pool_kernel(u_ref, hist_ref, pw_ref, ps_ref, o_ref, ext_ref, *, tp, pos0):
    i = pl.program_id(1)
    hrows = POOL_HIST + 1

    @pl.when(i == 0)
    def _():
        ext_ref[0:hrows, :] = hist_ref[0]

    @pl.when(i > 0)
    def _():
        ext_ref[0:hrows, :] = ext_ref[tp:tp + hrows, :]

    ext_ref[hrows:hrows + tp, :] = u_ref[0]
    row = lax.broadcasted_iota(jnp.int32, (tp, 1), 0)
    avail = i * tp + row + pos0
    for gi, w in enumerate(POOL_SIZES):
        c0 = gi * POOL_GROUP
        u = ext_ref[hrows:hrows + tp, c0:c0 + POOL_GROUP]
        acc = u
        for j in range(1, w):
            acc = acc + ext_ref[hrows - j:hrows - j + tp, c0:c0 + POOL_GROUP]
        cnt = jnp.maximum(jnp.minimum(avail, w), 1).astype(F32)
        pooled = acc / cnt - u
        mixed = jnp.dot(pooled.astype(BF16), pw_ref[gi], preferred_element_type=F32)
        o_ref[0, :, c0:c0 + POOL_GROUP] = (mixed * ps_ref[:, c0:c0 + POOL_GROUP]).astype(o_ref.dtype)


def _pool(proj3, hist16, pw_bf16, pscale, tp, pos0):
    b, l, _ = proj3.shape
    return pl.pallas_call(
        functools.partial(_pool_kernel, tp=tp, pos0=pos0),
        out_shape=jax.ShapeDtypeStruct((b, l, W_POOL), BF16),
        grid=(b, l // tp),
        in_specs=[
            pl.BlockSpec((1, tp, W_POOL), lambda s, i: (s, i, 0)),
            pl.BlockSpec((1, POOL_HIST + 1, W_POOL), lambda s, i: (s, 0, 0)),
            _const_spec((len(POOL_SIZES), POOL_GROUP, POOL_GROUP)),
            _const_spec((1, W_POOL)),
        ],
        out_specs=pl.BlockSpec((1, tp, W_POOL), lambda s, i: (s, i, 0)),
        scratch_shapes=[pltpu.VMEM((POOL_HIST + 1 + tp, W_POOL), F32)],
        compiler_params=_params("parallel", "arbitrary"),
        name="pool",
    )(proj3, hist16, pw_bf16, pscale)


def _s5_kernel(u_ref, h0r_ref, h0i_ref, bblk_ref, cblk_ref, sc_ref, d_ref, wglu_ref,
               o_ref, hr_out, hi_out, bu_ref, y_ref, cr_ref, ci_ref, *, ts):
    i = pl.program_id(1)

    @pl.when(i == 0)
    def _():
        cr_ref[...] = h0r_ref[0]
        ci_ref[...] = h0i_ref[0]

    u = u_ref[0]
    ub = u.astype(BF16)
    for sb in range(SSM_SB):
        bu_ref[...] = jnp.dot(ub[:, sb * SSM_SB_CH:(sb + 1) * SSM_SB_CH], bblk_ref[sb],
                              preferred_element_type=F32)
        for lc in range(SSM_SB_LANES // SCAN_LANES):
            re = slice(lc * SCAN_LANES, (lc + 1) * SCAN_LANES)
            im = slice(SSM_SB_LANES + lc * SCAN_LANES, SSM_SB_LANES + (lc + 1) * SCAN_LANES)
            a1r, a1i, a2r, a2i, a4r, a4i, pr, pi = [sc_ref[sb, k, :, re] for k in range(8)]
            steps = ((a1r, a1i, 1), (a2r, a2i, 2), (a4r, a4i, 4))
            cr0 = jnp.broadcast_to(cr_ref[sb:sb + 1, re], (SCAN_ROWS, SCAN_LANES))
            ci0 = jnp.broadcast_to(ci_ref[sb:sb + 1, re], (SCAN_ROWS, SCAN_LANES))

            def body(r, carry, re=re, im=im, steps=steps, pr=pr, pi=pi):
                cr, ci = carry
                rows = pl.ds(pl.multiple_of(r * SCAN_ROWS, SCAN_ROWS), SCAN_ROWS)
                xr = bu_ref[rows, re]
                xi = bu_ref[rows, im]
                for ar, ai, d in steps:
                    sr = pltpu.roll(xr, d, 0)
                    si = pltpu.roll(xi, d, 0)
                    xr, xi = xr + ar * sr - ai * si, xi + ar * si + ai * sr
                hr = xr + pr * cr - pi * ci
                hi = xi + pr * ci + pi * cr
                bu_ref[rows, re] = hr
                bu_ref[rows, im] = hi
                last = slice(SCAN_ROWS - 1, SCAN_ROWS)
                return (jnp.broadcast_to(hr[last], (SCAN_ROWS, SCAN_LANES)),
                        jnp.broadcast_to(hi[last], (SCAN_ROWS, SCAN_LANES)))

            cr, ci = lax.fori_loop(0, ts // SCAN_ROWS, body, (cr0, ci0))
            cr_ref[sb:sb + 1, re] = cr[0:1]
            ci_ref[sb:sb + 1, re] = ci[0:1]
        y_ref[:, sb * SSM_SB_CH:(sb + 1) * SSM_SB_CH] = jnp.dot(
            bu_ref[...].astype(BF16), cblk_ref[sb], preferred_element_type=F32)
    y = jax.nn.gelu(y_ref[...] + d_ref[...] * u)
    z = jnp.dot(y.astype(BF16), wglu_ref[...], preferred_element_type=F32)
    o_ref[0] = (y * jax.nn.sigmoid(z)).astype(o_ref.dtype)

    @pl.when(i == pl.num_programs(1) - 1)
    def _():
        hr_out[0] = cr_ref[...]
        hi_out[0] = ci_ref[...]


def _s5(proj3, h0r, h0i, bblk, cblk, scan_consts, d_skip, wglu_bf16, ts):
    b, l, _ = proj3.shape
    st_spec = pl.BlockSpec((1, SSM_SB, SSM_SB_LANES), lambda s, i: (s, 0, 0))
    st_shape = jax.ShapeDtypeStruct((b, SSM_SB, SSM_SB_LANES), F32)
    return pl.pallas_call(
        functools.partial(_s5_kernel, ts=ts),
        out_shape=(jax.ShapeDtypeStruct((b, l, W_SSM), BF16), st_shape, st_shape),
        grid=(b, l // ts),
        in_specs=[
            pl.BlockSpec((1, ts, W_SSM), lambda s, i: (s, i, 1)),
            st_spec, st_spec,
            _const_spec(bblk.shape), _const_spec(cblk.shape), _const_spec(scan_consts.shape),
            _const_spec((1, W_SSM)), _const_spec((W_SSM, W_SSM)),
        ],
        out_specs=(pl.BlockSpec((1, ts, W_SSM), lambda s, i: (s, i, 0)), st_spec, st_spec),
        scratch_shapes=[
            pltpu.VMEM((ts, 2 * SSM_SB_LANES), F32),
            pltpu.VMEM((ts, W_SSM), F32),
            pltpu.VMEM((SSM_SB, SSM_SB_LANES), F32),
            pltpu.VMEM((SSM_SB, SSM_SB_LANES), F32),
        ],
        compiler_params=_params("parallel", "arbitrary"),
        name="s5",
    )(proj3, h0r, h0i, bblk, cblk, scan_consts, d_skip, wglu_bf16)


def _s5_weights(a_re, a_im, log_step, b_re, b_im, c_re, c_im):
    dt = jnp.exp(log_step)[:, None]
    mag = jnp.exp(a_re * dt)
    abr = mag * jnp.cos(a_im * dt)
    abi = mag * jnp.sin(a_im * dt)
    nr, ni = abr - 1.0, abi
    den = a_re * a_re + a_im * a_im
    f_re = (nr * a_re + ni * a_im) / den
    f_im = (ni * a_re - nr * a_im) / den
    bbr = f_re[..., None] * b_re - f_im[..., None] * b_im
    bbi = f_re[..., None] * b_im + f_im[..., None] * b_re
    gl = N_SSM_GROUPS // SSM_SB
    eye = jnp.eye(gl, dtype=F32)

    def bdiag_in(x):
        x = x.reshape(SSM_SB, gl, SSM_STATE, SSM_GROUP)
        return jnp.einsum("sgpc,gh->sgchp", x, eye).reshape(SSM_SB, gl * SSM_GROUP, gl * SSM_STATE)

    def bdiag_out(x):
        x = x.reshape(SSM_SB, gl, SSM_GROUP, SSM_STATE)
        return jnp.einsum("sgcp,gh->sgphc", x, eye).reshape(SSM_SB, gl * SSM_STATE, gl * SSM_GROUP)

    bblk = jnp.concatenate([bdiag_in(bbr), bdiag_in(bbi)], axis=2).astype(BF16)
    cblk = jnp.concatenate([bdiag_out(c_re), -bdiag_out(c_im)], axis=1).astype(BF16)

    ar = abr.reshape(SSM_SB, SSM_SB_LANES)
    ai = abi.reshape(SSM_SB, SSM_SB_LANES)
    pw_r, pw_i = [ar], [ai]
    for _ in range(SCAN_ROWS - 1):
        pr, pi = pw_r[-1], pw_i[-1]
        pw_r.append(pr * ar - pi * ai)
        pw_i.append(pr * ai + pi * ar)
    row = jnp.arange(SCAN_ROWS)[None, :, None]

    def step(d):
        m = (row >= d).astype(F32)
        return pw_r[d - 1][:, None, :] * m, pw_i[d - 1][:, None, :] * m

    s1r, s1i = step(1)
    s2r, s2i = step(2)
    s4r, s4i = step(4)
    car_r = jnp.stack(pw_r, axis=1)
    car_i = jnp.stack(pw_i, axis=1)
    consts = jnp.stack([s1r, s1i, s2r, s2i, s4r, s4i, car_r, car_i], axis=1)
    return bblk, cblk, consts


def _merge_kernel(x_ref, oa_ref, ob_ref, ga_ref, gb_ref, pa_ref, pb_ref, wo_ref, g2_ref,
                  x2_ref, xn_ref):
    ma = jnp.dot(oa_ref[0], pa_ref[...], preferred_element_type=F32)
    mb = jnp.dot(ob_ref[0], pb_ref[...], preferred_element_type=F32)
    merged = jax.nn.sigmoid(ga_ref[0]) * ma + jax.nn.sigmoid(gb_ref[0]) * mb
    x2 = x_ref[0] + jnp.dot(merged.astype(BF16), wo_ref[...], preferred_element_type=F32)
    x2_ref[0] = x2
    xn_ref[0] = _rms(x2, g2_ref[...]).astype(BF16)


def _merge(x3, oa3, ob3, proj3, pa, pb, wo, g2, tm, pad_blocks):
    b, l, _ = x3.shape
    off = pad_blocks
    return pl.pallas_call(
        _merge_kernel,
        out_shape=(jax.ShapeDtypeStruct((b, l, D_MODEL), F32), jax.ShapeDtypeStruct((b, l, D_MODEL), BF16)),
        grid=(b, l // tm),
        in_specs=[
            pl.BlockSpec((1, tm, D_MODEL), lambda s, i: (s, i, 0)),
            pl.BlockSpec((1, tm, W_POOL), lambda s, i: (s, i + off, 0)),
            pl.BlockSpec((1, tm, W_SSM), lambda s, i: (s, i + off, 0)),
            pl.BlockSpec((1, tm, D_MODEL), lambda s, i: (s, i + off, 1)),
            pl.BlockSpec((1, tm, D_MODEL), lambda s, i: (s, i + off, 2)),
            _const_spec((W_POOL, D_MODEL)), _const_spec((W_SSM, D_MODEL)),
            _const_spec((D_MODEL, D_MODEL)), _const_spec((1, D_MODEL)),
        ],
        out_specs=(pl.BlockSpec((1, tm, D_MODEL), lambda s, i: (s, i, 0)),
                   pl.BlockSpec((1, tm, D_MODEL), lambda s, i: (s, i, 0))),
        compiler_params=_params("parallel", "parallel"),
        name="merge",
    )(x3, oa3, ob3, proj3, proj3, pa, pb, wo, g2)


def _query_kernel(xn_ref, wq_ref, k1_ref, k2_ref, s_ref):
    q = jnp.dot(xn_ref[...], wq_ref[...], preferred_element_type=F32).astype(BF16)
    nt = (((1,), (1,)), ((), ()))
    for h in range(PEER_HEADS):
        c0 = h * D_QUERY
        s_ref[h, 0] = lax.dot_general(k1_ref[...], q[:, c0:c0 + D_HALF], nt, preferred_element_type=F32)
        s_ref[h, 1] = lax.dot_general(k2_ref[...], q[:, c0 + D_HALF:c0 + D_QUERY], nt,
                                      preferred_element_type=F32)


def _query(xn, wq, k1, k2, tm):
    n = xn.shape[0]
    return pl.pallas_call(
        _query_kernel,
        out_shape=jax.ShapeDtypeStruct((PEER_HEADS, 2, N_KEYS, n), F32),
        grid=(n // tm,),
        in_specs=[
            pl.BlockSpec((tm, D_MODEL), lambda i: (i, 0)),
            _const_spec((D_MODEL, PEER_HEADS * D_QUERY)),
            _const_spec((N_KEYS, D_HALF)), _const_spec((N_KEYS, D_HALF)),
        ],
        out_specs=pl.BlockSpec((PEER_HEADS, 2, N_KEYS, tm), lambda i: (0, 0, 0, i)),
        compiler_params=_params("parallel"),
        name="query",
    )(xn, wq, k1, k2)


def _top16(val):
    rows, lanes = val.shape
    rowid = lax.broadcasted_iota(jnp.int32, (rows, lanes), 0).astype(F32)
    kid = lax.broadcasted_iota(jnp.int32, (PEER_TOPK, lanes), 0)

    def body(r, carry):
        val, rank, tops = carry
        m = jnp.max(val, axis=0, keepdims=True)
        idx = jnp.min(jnp.where(val == m, rowid, float(rows)), axis=0, keepdims=True)
        hit = rowid == idx
        rank = jnp.where(hit, r.astype(F32), rank)
        val = jnp.where(hit, -jnp.inf, val)
        tops = jnp.where(kid == r, m, tops)
        return val, rank, tops

    init = (val, jnp.full((rows, lanes), float(PEER_TOPK), F32), jnp.zeros((PEER_TOPK, lanes), F32))
    _, rank, tops = lax.fori_loop(0, PEER_TOPK, body, init)
    return rank, tops


def _outer16(a, b, op):
    return jnp.concatenate([op(a[r:r + 1], b) for r in range(PEER_TOPK)], axis=0)


def _gates_kernel(s_ref, n1_ref, e1_ref, r2_ref, e2_ref):
    s1 = s_ref[0, 0]
    s2 = s_ref[0, 1]
    rank1, v1 = _top16(s1)
    rank2, v2 = _top16(s2)
    cand = _outer16(v1, v2, jnp.add)
    crank, _ = _top16(cand)
    sel = (crank < float(PEER_TOPK)).astype(F32)
    x1 = jnp.exp(v1 - v1[0:1])
    x2 = jnp.exp(v2 - v2[0:1])
    z = jnp.sum(sel * _outer16(x1, x2, jnp.multiply), axis=0, keepdims=True)
    n1 = jnp.zeros_like(s1)
    for r in range(PEER_TOPK):
        cnt = jnp.sum(sel[r * PEER_TOPK:(r + 1) * PEER_TOPK], axis=0, keepdims=True)
        n1 = jnp.where(rank1 == float(r), cnt, n1)
    n1_ref[0] = n1
    e1_ref[0] = jnp.exp(s1 - v1[0:1]) / z
    r2_ref[0] = rank2
    e2_ref[0] = jnp.exp(s2 - v2[0:1])


def _gates(scores, tl):
    n = scores.shape[-1]
    spec = pl.BlockSpec((1, N_KEYS, tl), lambda i, h: (h, 0, i))
    shp = jax.ShapeDtypeStruct((PEER_HEADS, N_KEYS, n), F32)
    return pl.pallas_call(
        _gates_kernel,
        out_shape=(shp, shp, shp, shp),
        grid=(n // tl, PEER_HEADS),
        in_specs=[pl.BlockSpec((1, 2, N_KEYS, tl), lambda i, h: (h, 0, 0, i))],
        out_specs=(spec, spec, spec, spec),
        compiler_params=_params("parallel", "parallel"),
        name="gates",
    )(scores)


PEER_ROWS = 16
LANE = 128


def _peer_kernel(xn_ref, x2_ref, u_ref, vt_ref, n1_ref, e1_ref, r2_ref, e2_ref, gf_ref,
                 o_ref, acc_ref, act_ref, coef_ref, *, ec, t):
    c = pl.program_id(1)

    @pl.when(c == 0)
    def _():
        acc_ref[...] = jnp.zeros_like(acc_ref)

    nt = (((1,), (1,)), ((), ()))
    act_ref[...] = jax.nn.gelu(lax.dot_general(u_ref[...], xn_ref[...], nt, preferred_element_type=F32))
    keys_per_chunk = ec // N_KEYS
    for j in range(keys_per_chunk):
        for lt in range(t // LANE):
            lanes = slice(lt * LANE, (lt + 1) * LANE)
            n1b = [jnp.broadcast_to(n1_ref[h, 0, j:j + 1, lanes], (PEER_ROWS, LANE)) for h in range(PEER_HEADS)]
            e1b = [jnp.broadcast_to(e1_ref[h, 0, j:j + 1, lanes], (PEER_ROWS, LANE)) for h in range(PEER_HEADS)]

            def body(ib, _, j=j, lanes=lanes, n1b=n1b, e1b=e1b):
                rows = pl.ds(pl.multiple_of(ib * PEER_ROWS, PEER_ROWS), PEER_ROWS)
                w = jnp.zeros((PEER_ROWS, LANE), F32)
                for h in range(PEER_HEADS):
                    w = w + jnp.where(r2_ref[h, rows, lanes] < n1b[h], e1b[h] * e2_ref[h, rows, lanes], 0.0)
                orow = pl.ds(pl.multiple_of(j * N_KEYS + ib * PEER_ROWS, PEER_ROWS), PEER_ROWS)
                coef_ref[orow, lanes] = (w * act_ref[orow, lanes]).astype(BF16)
                return 0

            lax.fori_loop(0, N_KEYS // PEER_ROWS, body, 0)
    acc_ref[...] += jnp.dot(vt_ref[...], coef_ref[...], preferred_element_type=F32)

    @pl.when(c == pl.num_programs(1) - 1)
    def _():
        o_ref[...] = _rms(x2_ref[...] + acc_ref[...].T, gf_ref[...])


def _peer(xn, x2, u_bf16, vt_bf16, n1, e1, r2, e2, gf, t, ec):
    n = xn.shape[0]
    gspec = pl.BlockSpec((PEER_HEADS, N_KEYS, t), lambda b, c: (0, 0, b))
    kc = ec // N_KEYS
    n1 = n1.reshape(PEER_HEADS, N_KEYS // kc, kc, n)
    e1 = e1.reshape(PEER_HEADS, N_KEYS // kc, kc, n)
    cspec = pl.BlockSpec((PEER_HEADS, 1, kc, t), lambda b, c: (0, c, 0, b))
    return pl.pallas_call(
        functools.partial(_peer_kernel, ec=ec, t=t),
        out_shape=jax.ShapeDtypeStruct((n, D_MODEL), F32),
        grid=(n // t, N_EXPERTS // ec),
        in_specs=[
            pl.BlockSpec((t, D_MODEL), lambda b, c: (b, 0)),
            pl.BlockSpec((t, D_MODEL), lambda b, c: (b, 0)),
            pl.BlockSpec((ec, D_MODEL), lambda b, c: (c, 0)),
            pl.BlockSpec((D_MODEL, ec), lambda b, c: (0, c)),
            cspec, cspec, gspec, gspec,
            _const_spec((1, D_MODEL)),
        ],
        out_specs=pl.BlockSpec((t, D_MODEL), lambda b, c: (b, 0)),
        scratch_shapes=[
            pltpu.VMEM((D_MODEL, t), F32),
            pltpu.VMEM((ec, t), F32),
            pltpu.VMEM((ec, t), BF16),
        ],
        compiler_params=_params("parallel", "arbitrary"),
        name="peer",
    )(xn, x2, u_bf16, vt_bf16, n1, e1, r2, e2, gf)


def _group(x_real, x_in, hist16, h0r, h0i, wts, *, pad_rows, pos0, t_in, t_seq, t_merge, t_tok):
    b, lp, _ = x_in.shape
    l = x_real.shape[1]
    proj = _inproj(x_in.reshape(b * lp, D_MODEL), wts["norm1_g"], wts["w_in"], t_in)
    proj3 = proj.reshape(b, lp, W_IN_COLS)
    out_a = _pool(proj3, hist16, wts["pool_w"], wts["pool_scale"], t_seq, pos0)
    out_b, hr, hi = _s5(proj3, h0r, h0i, wts["bblk"], wts["cblk"], wts["scan"], wts["ssm_d"], wts["w_glu"], t_seq)
    n = b * l
    if pad_rows == 0:
        x_real, out_a, out_b, projm = (a.reshape(1, n, a.shape[-1]) for a in (x_real, out_a, out_b, proj3))
    else:
        projm = proj3
    x2, xn = _merge(x_real, out_a, out_b, projm, wts["proj_a"], wts["proj_b"], wts["w_out"],
                    wts["norm2_g"], t_merge, pad_rows // t_merge)
    xn = xn.reshape(n, D_MODEL)
    scores = _query(xn, wts["peer_wq"], wts["peer_key1"], wts["peer_key2"], t_tok)
    n1, e1, r2, e2 = _gates(scores, LANE)
    y = _peer(xn, x2.reshape(n, D_MODEL), wts["peer_u"], wts["peer_vt"], n1, e1, r2, e2,
              wts["final_norm_g"], t_tok, 512)
    new_hist = proj3[:, lp - POOL_HIST:, :W_POOL]
    shp = (b, N_SSM_GROUPS, SSM_STATE)
    return y.reshape(b, l, D_MODEL), new_hist, hr.reshape(shp), hi.reshape(shp)


PROMPT_PAD = 512


def kernel(x_prompt, x_sample, cache_pool, state_ssm_re, state_ssm_im, meta_tokens, norm1_g, w_in, pool_w, pool_scale, ssm_a_re, ssm_a_im, ssm_log_step, ssm_b_re, ssm_b_im, ssm_c_re, ssm_c_im, ssm_d, w_glu, proj_a, proj_b, w_out, norm2_g, peer_wq, peer_key1, peer_key2, peer_u, peer_v, final_norm_g):
    depth = norm1_g.shape[0]
    assert depth == 1
    bp, seq, _ = x_prompt.shape
    bs, dseq, _ = x_sample.shape
    past_len = 1024

    bblk, cblk, scan = _s5_weights(ssm_a_re[0], ssm_a_im[0], ssm_log_step[0], ssm_b_re[0], ssm_b_im[0],
                                   ssm_c_re[0], ssm_c_im[0])
    wts = {
        "norm1_g": norm1_g[0][None], "w_in": w_in[0].astype(BF16),
        "pool_w": pool_w[0].astype(BF16), "pool_scale": pool_scale[0][None],
        "bblk": bblk, "cblk": cblk, "scan": scan, "ssm_d": ssm_d[0][None], "w_glu": w_glu[0].astype(BF16),
        "proj_a": proj_a[0].astype(BF16), "proj_b": proj_b[0].astype(BF16), "w_out": w_out[0].astype(BF16),
        "norm2_g": norm2_g[0][None], "peer_wq": peer_wq[0].astype(BF16),
        "peer_key1": peer_key1[0].astype(BF16), "peer_key2": peer_key2[0].astype(BF16),
        "peer_u": peer_u[0].astype(BF16), "peer_vt": peer_v[0].astype(BF16).T,
        "final_norm_g": final_norm_g[None],
    }

    zpad = jnp.zeros((bp, PROMPT_PAD - N_META, D_MODEL), x_prompt.dtype)
    meta = jnp.broadcast_to(meta_tokens[None], (bp, N_META, D_MODEL))
    xp = jnp.concatenate([zpad, meta, x_prompt], axis=1)
    zst = jnp.zeros((bp, SSM_SB, SSM_SB_LANES), F32)
    yp, pool_p, re_p, im_p = _group(
        x_prompt, xp, jnp.zeros((bp, POOL_HIST + 1, W_POOL), F32), zst, zst, wts,
        pad_rows=PROMPT_PAD, pos0=1 - (PROMPT_PAD - N_META), t_in=512, t_seq=256, t_merge=256, t_tok=512)

    hist16 = jnp.pad(cache_pool[0], ((0, 0), (1, 0), (0, 0)))
    ys, pool_s, re_s, im_s = _group(
        x_sample, x_sample, hist16,
        state_ssm_re[0].reshape(bs, SSM_SB, SSM_SB_LANES), state_ssm_im[0].reshape(bs, SSM_SB, SSM_SB_LANES),
        wts, pad_rows=0, pos0=1 + past_len, t_in=512, t_seq=dseq, t_merge=256, t_tok=512)

    return (yp, ys, pool_p[None], re_p[None], im_p[None], pool_s[None], re_s[None], im_s[None])
```

```python
import functools

import jax
import jax.numpy as jnp
from jax import lax
from jax.experimental import pallas as pl
from jax.experimental.pallas import tpu as pltpu

D_MODEL = 2048
N_META = 16
W_POOL = 1024
W_SSM = 1024
POOL_SIZES = (2, 4, 8, 16)
POOL_GROUP = 256
POOL_HIST = 15
N_SSM_GROUPS = 64
SSM_GROUP = 16
SSM_STATE = 64
PEER_HEADS = 8
PEER_TOPK = 16
N_KEYS = 128
N_EXPERTS = N_KEYS * N_KEYS
D_QUERY = 256
D_HALF = 128
W_IN_COLS = W_POOL + W_SSM + 2 * D_MODEL
NORM_EPS = 1e-6

F32 = jnp.float32
BF16 = jnp.bfloat16

SSM_SB = 4
SSM_SB_CH = W_SSM // SSM_SB
SSM_SB_LANES = (N_SSM_GROUPS // SSM_SB) * SSM_STATE
SCAN_ROWS = 8
SCAN_LANES = 256

VMEM_LIMIT = 56 * 1024 * 1024


def _const_spec(shape):
    nd = len(shape)
    return pl.BlockSpec(shape, lambda *_: (0,) * nd, pipeline_mode=pl.Buffered(1))


def _params(*sem):
    return pltpu.CompilerParams(dimension_semantics=sem, vmem_limit_bytes=VMEM_LIMIT)


def _rms(x, g):
    return x * lax.rsqrt(jnp.mean(x * x, axis=-1, keepdims=True) + NORM_EPS) * g


def _inproj_kernel(x_ref, g_ref, w_ref, o_ref, xn_ref):
    @pl.when(pl.program_id(1) == 0)
    def _():
        xn_ref[...] = _rms(x_ref[...], g_ref[...]).astype(BF16)

    o_ref[...] = jnp.dot(xn_ref[...], w_ref[...], preferred_element_type=F32)


def _inproj(x, g, w_bf16, tm):
    n = x.shape[0]
    cb = 2048
    return pl.pallas_call(
        _inproj_kernel,
        out_shape=jax.ShapeDtypeStruct((n, W_IN_COLS), F32),
        grid=(n // tm, W_IN_COLS // cb),
        in_specs=[
            pl.BlockSpec((tm, D_MODEL), lambda i, j: (i, 0)),
            _const_spec((1, D_MODEL)),
            pl.BlockSpec((D_MODEL, cb), lambda i, j: (0, j)),
        ],
        out_specs=pl.BlockSpec((tm, cb), lambda i, j: (i, j)),
        scratch_shapes=[pltpu.VMEM((tm, D_MODEL), BF16)],
        compiler_params=_params("parallel", "arbitrary"),
        name="inproj",
    )(x, g, w_bf16)


def _pool_kernel(u_ref, hist_ref, pw_ref, ps_ref, o_ref, ext_ref, *, tp, pos0):
    i = pl.program_id(1)
    hrows = POOL_HIST + 1

    @pl.when(i == 0)
    def _():
        ext_ref[0:hrows, :] = hist_ref[0]

    @pl.when(i > 0)
    def _():
        ext_ref[0:hrows, :] = ext_ref[tp:tp + hrows, :]

    ext_ref[hrows:hrows + tp, :] = u_ref[0]
    row = lax.broadcasted_iota(jnp.int32, (tp, 1), 0)
    avail = i * tp + row + pos0
    for gi, w in enumerate(POOL_SIZES):
        c0 = gi * POOL_GROUP
        u = ext_ref[hrows:hrows + tp, c0:c0 + POOL_GROUP]
        acc = u
        for j in range(1, w):
            acc = acc + ext_ref[hrows - j:hrows - j + tp, c0:c0 + POOL_GROUP]
        cnt = jnp.maximum(jnp.minimum(avail, w), 1).astype(F32)
        pooled = acc / cnt - u
        mixed = jnp.dot(pooled.astype(BF16), pw_ref[gi], preferred_element_type=F32)
        o_ref[0, :, c0:c0 + POOL_GROUP] = (mixed * ps_ref[:, c0:c0 + POOL_GROUP]).astype(o_ref.dtype)


def _pool(proj3, hist16, pw_bf16, pscale, tp, pos0):
    b, l, _ = proj3.shape
    return pl.pallas_call(
        functools.partial(_pool_kernel, tp=tp, pos0=pos0),
        out_shape=jax.ShapeDtypeStruct((b, l, W_POOL), BF16),
        grid=(b, l // tp),
        in_specs=[
            pl.BlockSpec((1, tp, W_POOL), lambda s, i: (s, i, 0)),
            pl.BlockSpec((1, POOL_HIST + 1, W_POOL), lambda s, i: (s, 0, 0)),
            _const_spec((len(POOL_SIZES), POOL_GROUP, POOL_GROUP)),
            _const_spec((1, W_POOL)),
        ],
        out_specs=pl.BlockSpec((1, tp, W_POOL), lambda s, i: (s, i, 0)),
        scratch_shapes=[pltpu.VMEM((POOL_HIST + 1 + tp, W_POOL), F32)],
        compiler_params=_params("parallel", "arbitrary"),
        name="pool",
    )(proj3, hist16, pw_bf16, pscale)


def _s5_kernel(u_ref, h0r_ref, h0i_ref, bblk_ref, cblk_ref, sc_ref, d_ref, wglu_ref,
               o_ref, hr_out, hi_out, bu_ref, y_ref, cr_ref, ci_ref, *, ts):
    i = pl.program_id(1)

    @pl.when(i == 0)
    def _():
        cr_ref[...] = h0r_ref[0]
        ci_ref[...] = h0i_ref[0]

    u = u_ref[0]
    ub = u.astype(BF16)
    for sb in range(SSM_SB):
        bu_ref[...] = jnp.dot(ub[:, sb * SSM_SB_CH:(sb + 1) * SSM_SB_CH], bblk_ref[sb],
                              preferred_element_type=F32)
        for lc in range(SSM_SB_LANES // SCAN_LANES):
            re = slice(lc * SCAN_LANES, (lc + 1) * SCAN_LANES)
            im = slice(SSM_SB_LANES + lc * SCAN_LANES, SSM_SB_LANES + (lc + 1) * SCAN_LANES)
            a1r, a1i, a2r, a2i, a4r, a4i, pr, pi = [sc_ref[sb, k, :, re] for k in range(8)]
            steps = ((a1r, a1i, 1), (a2r, a2i, 2), (a4r, a4i, 4))
            cr0 = jnp.broadcast_to(cr_ref[sb:sb + 1, re], (SCAN_ROWS, SCAN_LANES))
            ci0 = jnp.broadcast_to(ci_ref[sb:sb + 1, re], (SCAN_ROWS, SCAN_LANES))

            def body(r, carry, re=re, im=im, steps=steps, pr=pr, pi=pi):
                cr, ci = carry
                rows = pl.ds(pl.multiple_of(r * SCAN_ROWS, SCAN_ROWS), SCAN_ROWS)
                xr = bu_ref[rows, re]
                xi = bu_ref[rows, im]
                for ar, ai, d in steps:
                    sr = pltpu.roll(xr, d, 0)
                    si = pltpu.roll(xi, d, 0)
                    xr, xi = xr + ar * sr - ai * si, xi + ar * si + ai * sr
                hr = xr + pr * cr - pi * ci
                hi = xi + pr * ci + pi * cr
                bu_ref[rows, re] = hr
                bu_ref[rows, im] = hi
                last = slice(SCAN_ROWS - 1, SCAN_ROWS)
                return (jnp.broadcast_to(hr[last], (SCAN_ROWS, SCAN_LANES)),
                        jnp.broadcast_to(hi[last], (SCAN_ROWS, SCAN_LANES)))

            cr, ci = lax.fori_loop(0, ts // SCAN_ROWS, body, (cr0, ci0))
            cr_ref[sb:sb + 1, re] = cr[0:1]
            ci_ref[sb:sb + 1, re] = ci[0:1]
        y_ref[:, sb * SSM_SB_CH:(sb + 1) * SSM_SB_CH] = jnp.dot(
            bu_ref[...].astype(BF16), cblk_ref[sb], preferred_element_type=F32)
    y = jax.nn.gelu(y_ref[...] + d_ref[...] * u)
    z = jnp.dot(y.astype(BF16), wglu_ref[...], preferred_element_type=F32)
    o_ref[0] = (y * jax.nn.sigmoid(z)).astype(o_ref.dtype)

    @pl.when(i == pl.num_programs(1) - 1)
    def _():
        hr_out[0] = cr_ref[...]
        hi_out[0] = ci_ref[...]


def _s5(proj3, h0r, h0i, bblk, cblk, scan_consts, d_skip, wglu_bf16, ts):
    b, l, _ = proj3.shape
    st_spec = pl.BlockSpec((1, SSM_SB, SSM_SB_LANES), lambda s, i: (s, 0, 0))
    st_shape = jax.ShapeDtypeStruct((b, SSM_SB, SSM_SB_LANES), F32)
    return pl.pallas_call(
        functools.partial(_s5_kernel, ts=ts),
        out_shape=(jax.ShapeDtypeStruct((b, l, W_SSM), BF16), st_shape, st_shape),
        grid=(b, l // ts),
        in_specs=[
            pl.BlockSpec((1, ts, W_SSM), lambda s, i: (s, i, 1)),
            st_spec, st_spec,
            _const_spec(bblk.shape), _const_spec(cblk.shape), _const_spec(scan_consts.shape),
            _const_spec((1, W_SSM)), _const_spec((W_SSM, W_SSM)),
        ],
        out_specs=(pl.BlockSpec((1, ts, W_SSM), lambda s, i: (s, i, 0)), st_spec, st_spec),
        scratch_shapes=[
            pltpu.VMEM((ts, 2 * SSM_SB_LANES), F32),
            pltpu.VMEM((ts, W_SSM), F32),
            pltpu.VMEM((SSM_SB, SSM_SB_LANES), F32),
            pltpu.VMEM((SSM_SB, SSM_SB_LANES), F32),
        ],
        compiler_params=_params("parallel", "arbitrary"),
        name="s5",
    )(proj3, h0r, h0i, bblk, cblk, scan_consts, d_skip, wglu_bf16)


def _s5_weights(a_re, a_im, log_step, b_re, b_im, c_re, c_im):
    dt = jnp.exp(log_step)[:, None]
    mag = jnp.exp(a_re * dt)
    abr = mag * jnp.cos(a_im * dt)
    abi = mag * jnp.sin(a_im * dt)
    nr, ni = abr - 1.0, abi
    den = a_re * a_re + a_im * a_im
    f_re = (nr * a_re + ni * a_im) / den
    f_im = (ni * a_re - nr * a_im) / den
    bbr = f_re[..., None] * b_re - f_im[..., None] * b_im
    bbi = f_re[..., None] * b_im + f_im[..., None] * b_re
    gl = N_SSM_GROUPS // SSM_SB
    eye = jnp.eye(gl, dtype=F32)

    def bdiag_in(x):
        x = x.reshape(SSM_SB, gl, SSM_STATE, SSM_GROUP)
        return jnp.einsum("sgpc,gh->sgchp", x, eye).reshape(SSM_SB, gl * SSM_GROUP, gl * SSM_STATE)

    def bdiag_out(x):
        x = x.reshape(SSM_SB, gl, SSM_GROUP, SSM_STATE)
        return jnp.einsum("sgcp,gh->sgphc", x, eye).reshape(SSM_SB, gl * SSM_STATE, gl * SSM_GROUP)

    bblk = jnp.concatenate([bdiag_in(bbr), bdiag_in(bbi)], axis=2).astype(BF16)
    cblk = jnp.concatenate([bdiag_out(c_re), -bdiag_out(c_im)], axis=1).astype(BF16)

    ar = abr.reshape(SSM_SB, SSM_SB_LANES)
    ai = abi.reshape(SSM_SB, SSM_SB_LANES)
    pw_r, pw_i = [ar], [ai]
    for _ in range(SCAN_ROWS - 1):
        pr, pi = pw_r[-1], pw_i[-1]
        pw_r.append(pr * ar - pi * ai)
        pw_i.append(pr * ai + pi * ar)
    row = jnp.arange(SCAN_ROWS)[None, :, None]

    def step(d):
        m = (row >= d).astype(F32)
        return pw_r[d - 1][:, None, :] * m, pw_i[d - 1][:, None, :] * m

    s1r, s1i = step(1)
    s2r, s2i = step(2)
    s4r, s4i = step(4)
    car_r = jnp.stack(pw_r, axis=1)
    car_i = jnp.stack(pw_i, axis=1)
    consts = jnp.stack([s1r, s1i, s2r, s2i, s4r, s4i, car_r, car_i], axis=1)
    return bblk, cblk, consts


def _merge_kernel(x_ref, oa_ref, ob_ref, ga_ref, gb_ref, pa_ref, pb_ref, wo_ref, g2_ref,
                  x2_ref, xnt_ref):
    ma = jnp.dot(oa_ref[0], pa_ref[...], preferred_element_type=F32)
    mb = jnp.dot(ob_ref[0], pb_ref[...], preferred_element_type=F32)
    merged = jax.nn.sigmoid(ga_ref[0]) * ma + jax.nn.sigmoid(gb_ref[0]) * mb
    x2 = x_ref[0] + jnp.dot(merged.astype(BF16), wo_ref[...], preferred_element_type=F32)
    x2_ref[0] = x2
    xnt_ref[...] = _rms(x2, g2_ref[...]).T.astype(BF16)


def _merge(x3, oa3, ob3, proj3, pa, pb, wo, g2, tm, pad_blocks):
    b, l, _ = x3.shape
    off = pad_blocks
    nblk = l // tm
    return pl.pallas_call(
        _merge_kernel,
        out_shape=(jax.ShapeDtypeStruct((b, l, D_MODEL), F32), jax.ShapeDtypeStruct((D_MODEL, b * l), BF16)),
        grid=(b, nblk),
        in_specs=[
            pl.BlockSpec((1, tm, D_MODEL), lambda s, i: (s, i, 0)),
            pl.BlockSpec((1, tm, W_POOL), lambda s, i: (s, i + off, 0)),
            pl.BlockSpec((1, tm, W_SSM), lambda s, i: (s, i + off, 0)),
            pl.BlockSpec((1, tm, D_MODEL), lambda s, i: (s, i + off, 1)),
            pl.BlockSpec((1, tm, D_MODEL), lambda s, i: (s, i + off, 2)),
            _const_spec((W_POOL, D_MODEL)), _const_spec((W_SSM, D_MODEL)),
            _const_spec((D_MODEL, D_MODEL)), _const_spec((1, D_MODEL)),
        ],
        out_specs=(pl.BlockSpec((1, tm, D_MODEL), lambda s, i: (s, i, 0)),
                   pl.BlockSpec((D_MODEL, tm), lambda s, i: (0, s * nblk + i))),
        compiler_params=_params("parallel", "parallel"),
        name="merge",
    )(x3, oa3, ob3, proj3, proj3, pa, pb, wo, g2)


def _query_kernel(xnt_ref, wqt_ref, k1_ref, k2_ref, s_ref):
    qt = jnp.dot(wqt_ref[...], xnt_ref[...], preferred_element_type=F32).astype(BF16)
    for h in range(PEER_HEADS):
        r0 = h * D_QUERY
        s_ref[h, 0] = jnp.dot(k1_ref[...], qt[r0:r0 + D_HALF], preferred_element_type=F32)
        s_ref[h, 1] = jnp.dot(k2_ref[...], qt[r0 + D_HALF:r0 + D_QUERY], preferred_element_type=F32)


def _query(xnt, wqt, k1, k2, tm):
    n = xnt.shape[1]
    return pl.pallas_call(
        _query_kernel,
        out_shape=jax.ShapeDtypeStruct((PEER_HEADS, 2, N_KEYS, n), F32),
        grid=(n // tm,),
        in_specs=[
            pl.BlockSpec((D_MODEL, tm), lambda i: (0, i)),
            _const_spec((PEER_HEADS * D_QUERY, D_MODEL)),
            _const_spec((N_KEYS, D_HALF)), _const_spec((N_KEYS, D_HALF)),
        ],
        out_specs=pl.BlockSpec((PEER_HEADS, 2, N_KEYS, tm), lambda i: (0, 0, 0, i)),
        compiler_params=_params("parallel"),
        name="query",
    )(xnt, wqt, k1, k2)


LANE = 128
SUBLANE = 8
GATE_TOKENS = SUBLANE * LANE


def _oddeven_merge_sort_pairs(n):
    pairs = []
    p = 1
    while p < n:
        k = p
        while k >= 1:
            for j in range(k % p, n - k, 2 * k):
                for i in range(min(k, n - j - k)):
                    if (i + j) // (p * 2) == (i + j + k) // (p * 2):
                        pairs.append((i + j, i + j + k))
            k //= 2
        p *= 2
    return pairs


_SORT16_PAIRS = _oddeven_merge_sort_pairs(PEER_TOPK)


def _cmpx(v, i, j):
    a, b = v[i], v[j]
    v[i] = jnp.maximum(a, b)
    v[j] = jnp.minimum(a, b)


def _sort16(v):
    v = list(v)
    for i, j in _SORT16_PAIRS:
        _cmpx(v, i, j)
    return v


def _merge_top16(a, b):
    k = PEER_TOPK
    c = [a[i] if b[k - 1 - i] is None else jnp.maximum(a[i], b[k - 1 - i]) for i in range(k)]
    d = k // 2
    while d >= 1:
        for i in range(k):
            if not i & d:
                _cmpx(c, i, i + d)
        d //= 2
    return c


def _top16_values(rows):
    groups = [_sort16(rows[g:g + PEER_TOPK]) for g in range(0, len(rows), PEER_TOPK)]
    while len(groups) > 1:
        groups = [_merge_top16(groups[g], groups[g + 1]) for g in range(0, len(groups), 2)]
    return groups[0]


def _top16(val):
    rows = val.shape[0]
    rowid = lax.broadcasted_iota(jnp.int32, val.shape, 0).astype(F32)
    kshape = (PEER_TOPK,) + val.shape[1:]
    kid = lax.broadcasted_iota(jnp.int32, kshape, 0)

    def body(r, carry):
        val, rank, tops = carry
        m = jnp.max(val, axis=0, keepdims=True)
        idx = jnp.min(jnp.where(val == m, rowid, float(rows)), axis=0, keepdims=True)
        hit = rowid == idx
        rank = jnp.where(hit, r.astype(F32), rank)
        val = jnp.where(hit, -jnp.inf, val)
        tops = jnp.where(kid == r, m, tops)
        return val, rank, tops

    init = (val, jnp.full(val.shape, float(PEER_TOPK), F32), jnp.zeros(kshape, F32))
    _, rank, tops = lax.fori_loop(0, PEER_TOPK, body, init)
    return rank, tops


def _outer16(a, b, op):
    return jnp.concatenate([op(a[r:r + 1], b) for r in range(PEER_TOPK)], axis=0)


def _gates_kernel(s_ref, a1_ref, e1_ref, a2_ref, e2_ref, tau_ref):
    k = PEER_TOPK
    s1 = [s_ref[0, 0, i] for i in range(N_KEYS)]
    s2 = [s_ref[0, 1, i] for i in range(N_KEYS)]
    v1 = _top16_values(s1)
    v2 = _top16_values(s2)
    width = [k // (r + 1) for r in range(k)]
    cell = [[v1[r1] + v2[r2] for r2 in range(width[r1])] for r1 in range(k)]
    top = cell[0]
    for r1 in range(1, k):
        top = _merge_top16(top, cell[r1] + [None] * (k - width[r1]))
    tau = top[k - 1]

    def count(vals, thr):
        n = jnp.zeros_like(thr)
        for v in vals:
            n = n + (v >= thr).astype(F32)
        return n

    x1 = [jnp.exp(v - v1[0]) for v in v1]
    x2 = [jnp.exp(v - v2[0]) for v in v2]
    z = jnp.zeros_like(tau)
    ncell = jnp.zeros_like(tau)
    for r1 in range(k):
        for r2 in range(width[r1]):
            hit = cell[r1][r2] >= tau
            ncell = ncell + hit.astype(F32)
            z = z + jnp.where(hit, x1[r1] * x2[r2], 0.0)
    ambiguous = (count(s1, v1[k - 1]) != float(k)) | (count(s2, v2[k - 1]) != float(k)) | (ncell != float(k))
    for r1 in range(1, k):
        ambiguous = ambiguous | (v1[r1] + v2[width[r1]] >= tau)
    inv_z = 1.0 / z
    for i in range(N_KEYS):
        a1_ref[0, i] = s1[i]
        a2_ref[0, i] = s2[i]
        e1_ref[0, i] = jnp.where(s1[i] >= v1[k - 1], jnp.exp(s1[i] - v1[0]) * inv_z, 0.0)
        e2_ref[0, i] = jnp.where(s2[i] >= v2[k - 1], jnp.exp(s2[i] - v2[0]), 0.0)
    tau_ref[0, 0] = tau

    @pl.when(jnp.max(ambiguous.astype(F32)) > 0.0)
    def _():
        t1 = s_ref[0, 0]
        t2 = s_ref[0, 1]
        rank1, w1 = _top16(t1)
        rank2, w2 = _top16(t2)
        crank, _ = _top16(_outer16(w1, w2, jnp.add))
        sel = (crank < float(k)).astype(F32)
        y1 = jnp.exp(w1 - w1[0:1])
        y2 = jnp.exp(w2 - w2[0:1])
        zz = jnp.sum(sel * _outer16(y1, y2, jnp.multiply), axis=0, keepdims=True)
        n1 = jnp.zeros_like(t1)
        for r in range(k):
            cnt = jnp.sum(sel[r * k:(r + 1) * k], axis=0, keepdims=True)
            n1 = jnp.where(rank1 == float(r), cnt, n1)
        a1_ref[0] = n1
        a2_ref[0] = -rank2
        e1_ref[0] = jnp.exp(t1 - w1[0:1]) / zz
        e2_ref[0] = jnp.exp(t2 - w2[0:1])
        tau_ref[0, 0] = jnp.full((SUBLANE, LANE), 0.5, F32)


def _gates(scores):
    n = scores.shape[-1]
    nt = n // LANE
    s5d = scores.reshape(PEER_HEADS, 2, N_KEYS, nt, LANE)
    spec = pl.BlockSpec((1, N_KEYS, SUBLANE, LANE), lambda i, h: (h, 0, i, 0))
    shp = jax.ShapeDtypeStruct((PEER_HEADS, N_KEYS, nt, LANE), F32)
    a1, e1, a2, e2, tau = pl.pallas_call(
        _gates_kernel,
        out_shape=(shp, shp, shp, shp, jax.ShapeDtypeStruct((PEER_HEADS, 1, nt, LANE), F32)),
        grid=(n // GATE_TOKENS, PEER_HEADS),
        in_specs=[pl.BlockSpec((1, 2, N_KEYS, SUBLANE, LANE), lambda i, h: (h, 0, 0, i, 0))],
        out_specs=(spec, spec, spec, spec, pl.BlockSpec((1, 1, SUBLANE, LANE), lambda i, h: (h, 0, i, 0))),
        compiler_params=_params("parallel", "parallel"),
        name="gates",
    )(s5d)
    flat = lambda a: a.reshape(PEER_HEADS, a.shape[1], n)
    return flat(a1), flat(e1), flat(a2), flat(e2), flat(tau)


PEER_ROWS = 16


def _peer_kernel(xnt_ref, x2_ref, u_ref, vt_ref, a1_ref, e1_ref, a2_ref, e2_ref, tau_ref, gf_ref,
                 o_ref, acc_ref, act_ref, coef_ref, *, ec, t):
    c = pl.program_id(1)
    slot = c % 2

    @pl.when(c == 0)
    def _():
        coef_ref[1] = jnp.zeros((ec, t), BF16)

    prev = jnp.where(c == 0, 0.0, acc_ref[...])
    acc_ref[...] = prev + jnp.dot(vt_ref[...], coef_ref[1 - slot], preferred_element_type=F32)
    act_ref[...] = jax.nn.gelu(jnp.dot(u_ref[...], xnt_ref[...], preferred_element_type=F32))
    for lt in range(t // LANE):
        lanes = slice(lt * LANE, (lt + 1) * LANE)
        taub = [jnp.broadcast_to(tau_ref[h, :, lanes], (SUBLANE, LANE)) for h in range(PEER_HEADS)]
        for j in range(ec // N_KEYS):
            a1b = [jnp.broadcast_to(a1_ref[h, 0, j:j + 1, lanes], (SUBLANE, LANE)) for h in range(PEER_HEADS)]
            e1b = [jnp.broadcast_to(e1_ref[h, 0, j:j + 1, lanes], (SUBLANE, LANE)) for h in range(PEER_HEADS)]
            for ib in range(N_KEYS // PEER_ROWS):
                halves = []
                for half in range(PEER_ROWS // SUBLANE):
                    r0 = ib * PEER_ROWS + half * SUBLANE
                    rows = slice(r0, r0 + SUBLANE)
                    w = jnp.zeros((SUBLANE, LANE), F32)
                    for h in range(PEER_HEADS):
                        hit = a1b[h] + a2_ref[h, rows, lanes] >= taub[h]
                        w = w + jnp.where(hit, e1b[h] * e2_ref[h, rows, lanes], 0.0)
                    halves.append(w * act_ref[j * N_KEYS + r0:j * N_KEYS + r0 + SUBLANE, lanes])
                orow = slice(j * N_KEYS + ib * PEER_ROWS, j * N_KEYS + (ib + 1) * PEER_ROWS)
                coef_ref[slot, orow, lanes] = jnp.concatenate(halves, axis=0).astype(BF16)

    @pl.when(c == pl.num_programs(1) - 1)
    def _():
        o_ref[...] = _rms(x2_ref[...] + acc_ref[...].T, gf_ref[...])


def _peer(xnt, x2, u_bf16, vt_bf16, a1, e1, a2, e2, tau, gf, t, ec):
    n = xnt.shape[1]
    nc = N_EXPERTS // ec
    kc = ec // N_KEYS
    a1 = a1.reshape(PEER_HEADS, nc, kc, n)
    e1 = e1.reshape(PEER_HEADS, nc, kc, n)
    cur = lambda c: jnp.minimum(c, nc - 1)
    cspec = pl.BlockSpec((PEER_HEADS, 1, kc, t), lambda b, c: (0, cur(c), 0, b))
    gspec = pl.BlockSpec((PEER_HEADS, N_KEYS, t), lambda b, c: (0, 0, b))
    return pl.pallas_call(
        functools.partial(_peer_kernel, ec=ec, t=t),
        out_shape=jax.ShapeDtypeStruct((n, D_MODEL), F32),
        grid=(n // t, nc + 1),
        in_specs=[
            pl.BlockSpec((D_MODEL, t), lambda b, c: (0, b)),
            pl.BlockSpec((t, D_MODEL), lambda b, c: (b, 0)),
            pl.BlockSpec((ec, D_MODEL), lambda b, c: (cur(c), 0)),
            pl.BlockSpec((D_MODEL, ec), lambda b, c: (0, jnp.maximum(c - 1, 0))),
            cspec, cspec, gspec, gspec,
            pl.BlockSpec((PEER_HEADS, 1, t), lambda b, c: (0, 0, b)),
            _const_spec((1, D_MODEL)),
        ],
        out_specs=pl.BlockSpec((t, D_MODEL), lambda b, c: (b, 0)),
        scratch_shapes=[
            pltpu.VMEM((D_MODEL, t), F32),
            pltpu.VMEM((ec, t), F32),
            pltpu.VMEM((2, ec, t), BF16),
        ],
        compiler_params=_params("parallel", "arbitrary"),
        name="peer",
    )(xnt, x2, u_bf16, vt_bf16, a1, e1, a2, e2, tau, gf)


def _group(x_real, x_in, hist16, h0r, h0i, wts, *, pad_rows, pos0, t_in, t_seq, t_merge, t_tok):
    b, lp, _ = x_in.shape
    l = x_real.shape[1]
    proj = _inproj(x_in.reshape(b * lp, D_MODEL), wts["norm1_g"], wts["w_in"], t_in)
    proj3 = proj.reshape(b, lp, W_IN_COLS)
    out_a = _pool(proj3, hist16, wts["pool_w"], wts["pool_scale"], t_seq, pos0)
    out_b, hr, hi = _s5(proj3, h0r, h0i, wts["bblk"], wts["cblk"], wts["scan"], wts["ssm_d"], wts["w_glu"], t_seq)
    n = b * l
    if pad_rows == 0:
        x_real, out_a, out_b, projm = (a.reshape(1, n, a.shape[-1]) for a in (x_real, out_a, out_b, proj3))
    else:
        projm = proj3
    x2, xnt = _merge(x_real, out_a, out_b, projm, wts["proj_a"], wts["proj_b"], wts["w_out"],
                     wts["norm2_g"], t_merge, pad_rows // t_merge)
    scores = _query(xnt, wts["peer_wqt"], wts["peer_key1"], wts["peer_key2"], t_tok)
    a1, e1, a2, e2, tau = _gates(scores)
    y = _peer(xnt, x2.reshape(n, D_MODEL), wts["peer_u"], wts["peer_vt"], a1, e1, a2, e2, tau,
              wts["final_norm_g"], t_tok, 512)
    new_hist = proj3[:, lp - POOL_HIST:, :W_POOL]
    shp = (b, N_SSM_GROUPS, SSM_STATE)
    return y.reshape(b, l, D_MODEL), new_hist, hr.reshape(shp), hi.reshape(shp)


PROMPT_PAD = 512


def kernel(x_prompt, x_sample, cache_pool, state_ssm_re, state_ssm_im, meta_tokens, norm1_g, w_in, pool_w, pool_scale, ssm_a_re, ssm_a_im, ssm_log_step, ssm_b_re, ssm_b_im, ssm_c_re, ssm_c_im, ssm_d, w_glu, proj_a, proj_b, w_out, norm2_g, peer_wq, peer_key1, peer_key2, peer_u, peer_v, final_norm_g):
    depth = norm1_g.shape[0]
    assert depth == 1
    bp, seq, _ = x_prompt.shape
    bs, dseq, _ = x_sample.shape
    past_len = 1024

    bblk, cblk, scan = _s5_weights(ssm_a_re[0], ssm_a_im[0], ssm_log_step[0], ssm_b_re[0], ssm_b_im[0],
                                   ssm_c_re[0], ssm_c_im[0])
    wts = {
        "norm1_g": norm1_g[0][None], "w_in": w_in[0].astype(BF16),
        "pool_w": pool_w[0].astype(BF16), "pool_scale": pool_scale[0][None],
        "bblk": bblk, "cblk": cblk, "scan": scan, "ssm_d": ssm_d[0][None], "w_glu": w_glu[0].astype(BF16),
        "proj_a": proj_a[0].astype(BF16), "proj_b": proj_b[0].astype(BF16), "w_out": w_out[0].astype(BF16),
        "norm2_g": norm2_g[0][None], "peer_wqt": peer_wq[0].astype(BF16).T,
        "peer_key1": peer_key1[0].astype(BF16), "peer_key2": peer_key2[0].astype(BF16),
        "peer_u": peer_u[0].astype(BF16), "peer_vt": peer_v[0].astype(BF16).T,
        "final_norm_g": final_norm_g[None],
    }

    zpad = jnp.zeros((bp, PROMPT_PAD - N_META, D_MODEL), x_prompt.dtype)
    meta = jnp.broadcast_to(meta_tokens[None], (bp, N_META, D_MODEL))
    xp = jnp.concatenate([zpad, meta, x_prompt], axis=1)
    zst = jnp.zeros((bp, SSM_SB, SSM_SB_LANES), F32)
    yp, pool_p, re_p, im_p = _group(
        x_prompt, xp, jnp.zeros((bp, POOL_HIST + 1, W_POOL), F32), zst, zst, wts,
        pad_rows=PROMPT_PAD, pos0=1 - (PROMPT_PAD - N_META), t_in=512, t_seq=256, t_merge=256, t_tok=512)

    hist16 = jnp.pad(cache_pool[0], ((0, 0), (1, 0), (0, 0)))
    ys, pool_s, re_s, im_s = _group(
        x_sample, x_sample, hist16,
        state_ssm_re[0].reshape(bs, SSM_SB, SSM_SB_LANES), state_ssm_im[0].reshape(bs, SSM_SB, SSM_SB_LANES),
        wts, pad_rows=0, pos0=1 + past_len, t_in=512, t_seq=dseq, t_merge=256, t_tok=512)

    return (yp, ys, pool_p[None], re_p[None], im_p[None], pool_s[None], re_s[None], im_s[None])
```

```python
import functools

import jax
import jax.numpy as jnp
from jax import lax
from jax.experimental import pallas as pl
from jax.experimental.pallas import tpu as pltpu

D_MODEL = 2048
N_META = 16
W_POOL = 1024
W_SSM = 1024
POOL_SIZES = (2, 4, 8, 16)
POOL_GROUP = 256
POOL_HIST = 15
N_SSM_GROUPS = 64
SSM_GROUP = 16
SSM_STATE = 64
PEER_HEADS = 8
PEER_TOPK = 16
N_KEYS = 128
N_EXPERTS = N_KEYS * N_KEYS
D_QUERY = 256
D_HALF = 128
W_IN_COLS = W_POOL + W_SSM + 2 * D_MODEL
NORM_EPS = 1e-6

F32 = jnp.float32
BF16 = jnp.bfloat16

SSM_SB = 4
SSM_SB_CH = W_SSM // SSM_SB
SSM_SB_LANES = (N_SSM_GROUPS // SSM_SB) * SSM_STATE
SCAN_ROWS = 8
SCAN_LANES = 256

VMEM_LIMIT = 56 * 1024 * 1024


def _const_spec(shape):
    nd = len(shape)
    return pl.BlockSpec(shape, lambda *_: (0,) * nd, pipeline_mode=pl.Buffered(1))


def _params(*sem, flags=None):
    return pltpu.CompilerParams(dimension_semantics=sem, vmem_limit_bytes=VMEM_LIMIT, flags=flags)


def _rms(x, g):
    return x * lax.rsqrt(jnp.mean(x * x, axis=-1, keepdims=True) + NORM_EPS) * g


def _inproj_kernel(x_ref, g_ref, w_ref, o_ref, xn_ref):
    @pl.when(pl.program_id(1) == 0)
    def _():
        xn_ref[...] = _rms(x_ref[...], g_ref[...]).astype(BF16)

    o_ref[...] = jnp.dot(xn_ref[...], w_ref[...], preferred_element_type=F32)


def _inproj(x, g, w_bf16, tm):
    n = x.shape[0]
    cb = 2048
    return pl.pallas_call(
        _inproj_kernel,
        out_shape=jax.ShapeDtypeStruct((n, W_IN_COLS), F32),
        grid=(n // tm, W_IN_COLS // cb),
        in_specs=[
            pl.BlockSpec((tm, D_MODEL), lambda i, j: (i, 0)),
            _const_spec((1, D_MODEL)),
            pl.BlockSpec((D_MODEL, cb), lambda i, j: (0, j)),
        ],
        out_specs=pl.BlockSpec((tm, cb), lambda i, j: (i, j)),
        scratch_shapes=[pltpu.VMEM((tm, D_MODEL), BF16)],
        compiler_params=_params("parallel", "arbitrary"),
        name="inproj",
    )(x, g, w_bf16)


def _pool_kernel(u_ref, hist_ref, pw_ref, ps_ref, o_ref, ext_ref, *, tp, pos0):
    i = pl.program_id(1)
    hrows = POOL_HIST + 1

    @pl.when(i == 0)
    def _():
        ext_ref[0:hrows, :] = hist_ref[0]

    @pl.when(i > 0)
    def _():
        ext_ref[0:hrows, :] = ext_ref[tp:tp + hrows, :]

    ext_ref[hrows:hrows + tp, :] = u_ref[0]
    row = lax.broadcasted_iota(jnp.int32, (tp, 1), 0)
    avail = i * tp + row + pos0
    for gi, w in enumerate(POOL_SIZES):
        c0 = gi * POOL_GROUP
        u = ext_ref[hrows:hrows + tp, c0:c0 + POOL_GROUP]
        acc = u
        for j in range(1, w):
            acc = acc + ext_ref[hrows - j:hrows - j + tp, c0:c0 + POOL_GROUP]
        cnt = jnp.maximum(jnp.minimum(avail, w), 1).astype(F32)
        pooled = acc / cnt - u
        mixed = jnp.dot(pooled.astype(BF16), pw_ref[gi], preferred_element_type=F32)
        o_ref[0, :, c0:c0 + POOL_GROUP] = (mixed * ps_ref[:, c0:c0 + POOL_GROUP]).astype(o_ref.dtype)


def _pool(proj3, hist16, pw_bf16, pscale, tp, pos0):
    b, l, _ = proj3.shape
    return pl.pallas_call(
        functools.partial(_pool_kernel, tp=tp, pos0=pos0),
        out_shape=jax.ShapeDtypeStruct((b, l, W_POOL), BF16),
        grid=(b, l // tp),
        in_specs=[
            pl.BlockSpec((1, tp, W_POOL), lambda s, i: (s, i, 0)),
            pl.BlockSpec((1, POOL_HIST + 1, W_POOL), lambda s, i: (s, 0, 0)),
            _const_spec((len(POOL_SIZES), POOL_GROUP, POOL_GROUP)),
            _const_spec((1, W_POOL)),
        ],
        out_specs=pl.BlockSpec((1, tp, W_POOL), lambda s, i: (s, i, 0)),
        scratch_shapes=[pltpu.VMEM((POOL_HIST + 1 + tp, W_POOL), F32)],
        compiler_params=_params("parallel", "arbitrary"),
        name="pool",
    )(proj3, hist16, pw_bf16, pscale)


def _s5_kernel(u_ref, h0r_ref, h0i_ref, bblk_ref, cblk_ref, sc_ref, d_ref, wglu_ref,
               o_ref, hr_out, hi_out, bu_ref, y_ref, cr_ref, ci_ref, *, ts):
    i = pl.program_id(1)

    @pl.when(i == 0)
    def _():
        cr_ref[...] = h0r_ref[0]
        ci_ref[...] = h0i_ref[0]

    u = u_ref[0]
    ub = u.astype(BF16)
    for sb in range(SSM_SB):
        bu_ref[...] = jnp.dot(ub[:, sb * SSM_SB_CH:(sb + 1) * SSM_SB_CH], bblk_ref[sb],
                              preferred_element_type=F32)
        for lc in range(SSM_SB_LANES // SCAN_LANES):
            re = slice(lc * SCAN_LANES, (lc + 1) * SCAN_LANES)
            im = slice(SSM_SB_LANES + lc * SCAN_LANES, SSM_SB_LANES + (lc + 1) * SCAN_LANES)
            a1r, a1i, a2r, a2i, a4r, a4i, pr, pi = [sc_ref[sb, k, :, re] for k in range(8)]
            steps = ((a1r, a1i, 1), (a2r, a2i, 2), (a4r, a4i, 4))
            cr0 = jnp.broadcast_to(cr_ref[sb:sb + 1, re], (SCAN_ROWS, SCAN_LANES))
            ci0 = jnp.broadcast_to(ci_ref[sb:sb + 1, re], (SCAN_ROWS, SCAN_LANES))

            def body(r, carry, re=re, im=im, steps=steps, pr=pr, pi=pi):
                cr, ci = carry
                rows = pl.ds(pl.multiple_of(r * SCAN_ROWS, SCAN_ROWS), SCAN_ROWS)
                xr = bu_ref[rows, re]
                xi = bu_ref[rows, im]
                for ar, ai, d in steps:
                    sr = pltpu.roll(xr, d, 0)
                    si = pltpu.roll(xi, d, 0)
                    xr, xi = xr + ar * sr - ai * si, xi + ar * si + ai * sr
                hr = xr + pr * cr - pi * ci
                hi = xi + pr * ci + pi * cr
                bu_ref[rows, re] = hr
                bu_ref[rows, im] = hi
                last = slice(SCAN_ROWS - 1, SCAN_ROWS)
                return (jnp.broadcast_to(hr[last], (SCAN_ROWS, SCAN_LANES)),
                        jnp.broadcast_to(hi[last], (SCAN_ROWS, SCAN_LANES)))

            cr, ci = lax.fori_loop(0, ts // SCAN_ROWS, body, (cr0, ci0))
            cr_ref[sb:sb + 1, re] = cr[0:1]
            ci_ref[sb:sb + 1, re] = ci[0:1]
        y_ref[:, sb * SSM_SB_CH:(sb + 1) * SSM_SB_CH] = jnp.dot(
            bu_ref[...].astype(BF16), cblk_ref[sb], preferred_element_type=F32)
    y = jax.nn.gelu(y_ref[...] + d_ref[...] * u)
    z = jnp.dot(y.astype(BF16), wglu_ref[...], preferred_element_type=F32)
    o_ref[0] = (y * jax.nn.sigmoid(z)).astype(o_ref.dtype)

    @pl.when(i == pl.num_programs(1) - 1)
    def _():
        hr_out[0] = cr_ref[...]
        hi_out[0] = ci_ref[...]


def _s5(proj3, h0r, h0i, bblk, cblk, scan_consts, d_skip, wglu_bf16, ts):
    b, l, _ = proj3.shape
    st_spec = pl.BlockSpec((1, SSM_SB, SSM_SB_LANES), lambda s, i: (s, 0, 0))
    st_shape = jax.ShapeDtypeStruct((b, SSM_SB, SSM_SB_LANES), F32)
    return pl.pallas_call(
        functools.partial(_s5_kernel, ts=ts),
        out_shape=(jax.ShapeDtypeStruct((b, l, W_SSM), BF16), st_shape, st_shape),
        grid=(b, l // ts),
        in_specs=[
            pl.BlockSpec((1, ts, W_SSM), lambda s, i: (s, i, 1)),
            st_spec, st_spec,
            _const_spec(bblk.shape), _const_spec(cblk.shape), _const_spec(scan_consts.shape),
            _const_spec((1, W_SSM)), _const_spec((W_SSM, W_SSM)),
        ],
        out_specs=(pl.BlockSpec((1, ts, W_SSM), lambda s, i: (s, i, 0)), st_spec, st_spec),
        scratch_shapes=[
            pltpu.VMEM((ts, 2 * SSM_SB_LANES), F32),
            pltpu.VMEM((ts, W_SSM), F32),
            pltpu.VMEM((SSM_SB, SSM_SB_LANES), F32),
            pltpu.VMEM((SSM_SB, SSM_SB_LANES), F32),
        ],
        compiler_params=_params("parallel", "arbitrary"),
        name="s5",
    )(proj3, h0r, h0i, bblk, cblk, scan_consts, d_skip, wglu_bf16)


def _s5_weights(a_re, a_im, log_step, b_re, b_im, c_re, c_im):
    dt = jnp.exp(log_step)[:, None]
    mag = jnp.exp(a_re * dt)
    abr = mag * jnp.cos(a_im * dt)
    abi = mag * jnp.sin(a_im * dt)
    nr, ni = abr - 1.0, abi
    den = a_re * a_re + a_im * a_im
    f_re = (nr * a_re + ni * a_im) / den
    f_im = (ni * a_re - nr * a_im) / den
    bbr = f_re[..., None] * b_re - f_im[..., None] * b_im
    bbi = f_re[..., None] * b_im + f_im[..., None] * b_re
    gl = N_SSM_GROUPS // SSM_SB
    eye = jnp.eye(gl, dtype=F32)

    def bdiag_in(x):
        x = x.reshape(SSM_SB, gl, SSM_STATE, SSM_GROUP)
        return jnp.einsum("sgpc,gh->sgchp", x, eye).reshape(SSM_SB, gl * SSM_GROUP, gl * SSM_STATE)

    def bdiag_out(x):
        x = x.reshape(SSM_SB, gl, SSM_GROUP, SSM_STATE)
        return jnp.einsum("sgcp,gh->sgphc", x, eye).reshape(SSM_SB, gl * SSM_STATE, gl * SSM_GROUP)

    bblk = jnp.concatenate([bdiag_in(bbr), bdiag_in(bbi)], axis=2).astype(BF16)
    cblk = jnp.concatenate([bdiag_out(c_re), -bdiag_out(c_im)], axis=1).astype(BF16)

    ar = abr.reshape(SSM_SB, SSM_SB_LANES)
    ai = abi.reshape(SSM_SB, SSM_SB_LANES)
    pw_r, pw_i = [ar], [ai]
    for _ in range(SCAN_ROWS - 1):
        pr, pi = pw_r[-1], pw_i[-1]
        pw_r.append(pr * ar - pi * ai)
        pw_i.append(pr * ai + pi * ar)
    row = jnp.arange(SCAN_ROWS)[None, :, None]

    def step(d):
        m = (row >= d).astype(F32)
        return pw_r[d - 1][:, None, :] * m, pw_i[d - 1][:, None, :] * m

    s1r, s1i = step(1)
    s2r, s2i = step(2)
    s4r, s4i = step(4)
    car_r = jnp.stack(pw_r, axis=1)
    car_i = jnp.stack(pw_i, axis=1)
    consts = jnp.stack([s1r, s1i, s2r, s2i, s4r, s4i, car_r, car_i], axis=1)
    return bblk, cblk, consts


def _merge_kernel(x_ref, oa_ref, ob_ref, ga_ref, gb_ref, pa_ref, pb_ref, wo_ref, g2_ref,
                  x2_ref, xnt_ref):
    ma = jnp.dot(oa_ref[0], pa_ref[...], preferred_element_type=F32)
    mb = jnp.dot(ob_ref[0], pb_ref[...], preferred_element_type=F32)
    merged = jax.nn.sigmoid(ga_ref[0]) * ma + jax.nn.sigmoid(gb_ref[0]) * mb
    x2 = x_ref[0] + jnp.dot(merged.astype(BF16), wo_ref[...], preferred_element_type=F32)
    x2_ref[0] = x2
    xnt_ref[...] = _rms(x2, g2_ref[...]).T.astype(BF16)


def _merge(x3, oa3, ob3, proj3, pa, pb, wo, g2, tm, pad_blocks):
    b, l, _ = x3.shape
    off = pad_blocks
    nblk = l // tm
    return pl.pallas_call(
        _merge_kernel,
        out_shape=(jax.ShapeDtypeStruct((b, l, D_MODEL), F32), jax.ShapeDtypeStruct((D_MODEL, b * l), BF16)),
        grid=(b, nblk),
        in_specs=[
            pl.BlockSpec((1, tm, D_MODEL), lambda s, i: (s, i, 0)),
            pl.BlockSpec((1, tm, W_POOL), lambda s, i: (s, i + off, 0)),
            pl.BlockSpec((1, tm, W_SSM), lambda s, i: (s, i + off, 0)),
            pl.BlockSpec((1, tm, D_MODEL), lambda s, i: (s, i + off, 1)),
            pl.BlockSpec((1, tm, D_MODEL), lambda s, i: (s, i + off, 2)),
            _const_spec((W_POOL, D_MODEL)), _const_spec((W_SSM, D_MODEL)),
            _const_spec((D_MODEL, D_MODEL)), _const_spec((1, D_MODEL)),
        ],
        out_specs=(pl.BlockSpec((1, tm, D_MODEL), lambda s, i: (s, i, 0)),
                   pl.BlockSpec((D_MODEL, tm), lambda s, i: (0, s * nblk + i))),
        compiler_params=_params("parallel", "parallel"),
        name="merge",
    )(x3, oa3, ob3, proj3, proj3, pa, pb, wo, g2)


def _query_kernel(xnt_ref, wqt_ref, k1_ref, k2_ref, s_ref):
    qt = jnp.dot(wqt_ref[...], xnt_ref[...], preferred_element_type=F32).astype(BF16)
    for h in range(PEER_HEADS):
        r0 = h * D_QUERY
        s_ref[h, 0] = jnp.dot(k1_ref[...], qt[r0:r0 + D_HALF], preferred_element_type=F32)
        s_ref[h, 1] = jnp.dot(k2_ref[...], qt[r0 + D_HALF:r0 + D_QUERY], preferred_element_type=F32)


def _query(xnt, wqt, k1, k2, tm):
    n = xnt.shape[1]
    return pl.pallas_call(
        _query_kernel,
        out_shape=jax.ShapeDtypeStruct((PEER_HEADS, 2, N_KEYS, n), F32),
        grid=(n // tm,),
        in_specs=[
            pl.BlockSpec((D_MODEL, tm), lambda i: (0, i)),
            _const_spec((PEER_HEADS * D_QUERY, D_MODEL)),
            _const_spec((N_KEYS, D_HALF)), _const_spec((N_KEYS, D_HALF)),
        ],
        out_specs=pl.BlockSpec((PEER_HEADS, 2, N_KEYS, tm), lambda i: (0, 0, 0, i)),
        compiler_params=_params("parallel"),
        name="query",
    )(xnt, wqt, k1, k2)


LANE = 128
SUBLANE = 8
PACK_ROWS = 16
GATE_TOKENS = SUBLANE * LANE


def _oddeven_merge_sort_pairs(n):
    pairs = []
    p = 1
    while p < n:
        k = p
        while k >= 1:
            for j in range(k % p, n - k, 2 * k):
                for i in range(min(k, n - j - k)):
                    if (i + j) // (p * 2) == (i + j + k) // (p * 2):
                        pairs.append((i + j, i + j + k))
            k //= 2
        p *= 2
    return pairs


_SORT16_PAIRS = _oddeven_merge_sort_pairs(PEER_TOPK)


def _cmpx(v, i, j):
    a, b = v[i], v[j]
    v[i] = jnp.maximum(a, b)
    v[j] = jnp.minimum(a, b)


def _sort16(v):
    v = list(v)
    for i, j in _SORT16_PAIRS:
        _cmpx(v, i, j)
    return v


def _merge_top16(a, b):
    k = PEER_TOPK
    c = [a[i] if b[k - 1 - i] is None else jnp.maximum(a[i], b[k - 1 - i]) for i in range(k)]
    d = k // 2
    while d >= 1:
        for i in range(k):
            if not i & d:
                _cmpx(c, i, i + d)
        d //= 2
    return c


def _top16_values(rows):
    groups = [_sort16(rows[g:g + PEER_TOPK]) for g in range(0, len(rows), PEER_TOPK)]
    while len(groups) > 1:
        groups = [_merge_top16(groups[g], groups[g + 1]) for g in range(0, len(groups), 2)]
    return groups[0]


def _top16(val):
    rows = val.shape[0]
    rowid = lax.broadcasted_iota(jnp.int32, val.shape, 0).astype(F32)
    kshape = (PEER_TOPK,) + val.shape[1:]
    kid = lax.broadcasted_iota(jnp.int32, kshape, 0)

    def body(r, carry):
        val, rank, tops = carry
        m = jnp.max(val, axis=0, keepdims=True)
        idx = jnp.min(jnp.where(val == m, rowid, float(rows)), axis=0, keepdims=True)
        hit = rowid == idx
        rank = jnp.where(hit, lax.convert_element_type(r, F32), rank)
        val = jnp.where(hit, -jnp.inf, val)
        tops = jnp.where(kid == r, m, tops)
        return val, rank, tops

    init = (val, jnp.full(val.shape, float(PEER_TOPK), F32), jnp.zeros(kshape, F32))
    _, rank, tops = lax.fori_loop(0, PEER_TOPK, body, init)
    return rank, tops


def _outer16(a, b, op):
    return jnp.concatenate([op(a[r:r + 1], b) for r in range(PEER_TOPK)], axis=0)


def _gates_kernel(s_ref, n1_ref, e1_ref, r2_ref, e2_ref):
    k = PEER_TOPK
    s1 = [s_ref[0, 0, i] for i in range(N_KEYS)]
    s2 = [s_ref[0, 1, i] for i in range(N_KEYS)]
    v1 = _top16_values(s1)
    v2 = _top16_values(s2)
    width = [k // (r + 1) for r in range(k)]
    cell = [[v1[r1] + v2[r2] for r2 in range(width[r1])] for r1 in range(k)]
    top = cell[0]
    for r1 in range(1, k):
        top = _merge_top16(top, cell[r1] + [None] * (k - width[r1]))
    tau = top[k - 1]

    def count(vals, thr, cmp):
        n = jnp.zeros_like(thr)
        for v in vals:
            n = n + cmp(v, thr).astype(F32)
        return n

    x1 = [jnp.exp(v - v1[0]) for v in v1]
    x2 = [jnp.exp(v - v2[0]) for v in v2]
    z = jnp.zeros_like(tau)
    nrow = []
    for r1 in range(k):
        n = jnp.zeros_like(tau)
        for r2 in range(width[r1]):
            hit = cell[r1][r2] >= tau
            n = n + hit.astype(F32)
            z = z + jnp.where(hit, x1[r1] * x2[r2], 0.0)
        nrow.append(n)
    ncell = nrow[0]
    for n in nrow[1:]:
        ncell = ncell + n
    ge = lambda a, b: a >= b
    ambiguous = ((count(s1, v1[k - 1], ge) != float(k)) | (count(s2, v2[k - 1], ge) != float(k))
                 | (ncell != float(k)))
    for r1 in range(1, k):
        ambiguous = ambiguous | (v1[r1] + v2[width[r1]] >= tau)
    col = []
    for j in range(1, k + 1):
        c = jnp.full_like(tau, jnp.inf)
        for r1 in range(k // j):
            c = jnp.minimum(c, jnp.where(nrow[r1] >= float(j), v1[r1], jnp.inf))
        col.append(c)
    inv_z = 1.0 / z
    gt = lambda a, b: a > b
    for i in range(N_KEYS):
        n1_ref[0, i] = count(col, s1[i], lambda c, s: s >= c)
        r2_ref[0, i] = count(v2, s2[i], gt)
        e1_ref[0, i] = jnp.where(s1[i] >= v1[k - 1], jnp.exp(s1[i] - v1[0]) * inv_z, 0.0)
        e2_ref[0, i] = jnp.where(s2[i] >= v2[k - 1], jnp.exp(s2[i] - v2[0]), 0.0)

    @pl.when(jnp.max(ambiguous.astype(F32)) > 0.0)
    def _():
        t1 = s_ref[0, 0]
        t2 = s_ref[0, 1]
        rank1, w1 = _top16(t1)
        rank2, w2 = _top16(t2)
        crank, _ = _top16(_outer16(w1, w2, jnp.add))
        sel = (crank < float(k)).astype(F32)
        y1 = jnp.exp(w1 - w1[0:1])
        y2 = jnp.exp(w2 - w2[0:1])
        zz = jnp.sum(sel * _outer16(y1, y2, jnp.multiply), axis=0, keepdims=True)
        n1 = jnp.zeros_like(t1)
        for r in range(k):
            cnt = jnp.sum(sel[r * k:(r + 1) * k], axis=0, keepdims=True)
            n1 = jnp.where(rank1 == float(r), cnt, n1)
        n1_ref[0] = n1
        r2_ref[0] = rank2
        e1_ref[0] = jnp.exp(t1 - w1[0:1]) / zz
        e2_ref[0] = jnp.exp(t2 - w2[0:1])


def _gates(scores):
    n = scores.shape[-1]
    nt = n // LANE
    rows = SUBLANE
    s5d = scores.reshape(PEER_HEADS, 2, N_KEYS, nt, LANE)
    spec = pl.BlockSpec((1, N_KEYS, rows, LANE), lambda i, h: (h, 0, i, 0))
    shp = jax.ShapeDtypeStruct((PEER_HEADS, N_KEYS, nt, LANE), F32)
    outs = pl.pallas_call(
        _gates_kernel,
        out_shape=(shp, shp, shp, shp),
        grid=(nt // rows, PEER_HEADS),
        in_specs=[pl.BlockSpec((1, 2, N_KEYS, rows, LANE), lambda i, h: (h, 0, 0, i, 0))],
        out_specs=(spec, spec, spec, spec),
        compiler_params=_params("parallel", "parallel"),
        name="gates",
    )(s5d)
    return tuple(a.reshape(PEER_HEADS, N_KEYS, n) for a in outs)


def _peer_kernel(xnt_ref, x2_ref, u_ref, vt_ref, n1_ref, e1_ref, r2_ref, e2_ref, gf_ref,
                 o_ref, acc_ref, w_ref, coef_ref, *, ec, t):
    c = pl.program_id(1)
    slot = c % 2

    @pl.when(c == 0)
    def _():
        coef_ref[1] = jnp.zeros((ec, t), BF16)

    kt = 256
    for lt in range(t // LANE):
        lanes = slice(lt * LANE, (lt + 1) * LANE)
        for j in range(ec // N_KEYS):
            n1b = [jnp.broadcast_to(n1_ref[h, 0, j:j + 1, lanes], (SUBLANE, LANE)) for h in range(PEER_HEADS)]
            e1b = [jnp.broadcast_to(e1_ref[h, 0, j:j + 1, lanes], (SUBLANE, LANE)) for h in range(PEER_HEADS)]
            for ib in range(N_KEYS // SUBLANE):
                rows = slice(ib * SUBLANE, (ib + 1) * SUBLANE)
                w = jnp.zeros((SUBLANE, LANE), F32)
                for h in range(PEER_HEADS):
                    hit = r2_ref[h, rows, lanes] < n1b[h]
                    w = w + jnp.where(hit, e1b[h] * e2_ref[h, rows, lanes], 0.0)
                w_ref[j * N_KEYS + ib * SUBLANE:j * N_KEYS + (ib + 1) * SUBLANE, lanes] = w
    prev = jnp.where(c == 0, 0.0, acc_ref[...])
    acc_ref[...] = prev + jnp.dot(vt_ref[:, 0:kt], coef_ref[1 - slot, 0:kt, :], preferred_element_type=F32)
    for k0 in range(kt, ec, kt):
        acc_ref[...] += jnp.dot(vt_ref[:, k0:k0 + kt], coef_ref[1 - slot, k0:k0 + kt, :],
                                preferred_element_type=F32)
    act = jax.nn.gelu(jnp.dot(u_ref[...], xnt_ref[...], preferred_element_type=F32))
    coef_ref[slot] = (w_ref[...] * act).astype(BF16)

    @pl.when(c == pl.num_programs(1) - 1)
    def _():
        o_ref[...] = _rms(x2_ref[...] + acc_ref[...].T, gf_ref[...])


def _peer(xnt, x2, u_bf16, vt_bf16, n1, e1, r2, e2, gf, t, ec):
    n = xnt.shape[1]
    nc = N_EXPERTS // ec
    kc = ec // N_KEYS
    n1 = n1.reshape(PEER_HEADS, nc, kc, n)
    e1 = e1.reshape(PEER_HEADS, nc, kc, n)
    cur = lambda c: jnp.minimum(c, nc - 1)
    prv = lambda c: jnp.maximum(c - 1, 0)
    cspec = pl.BlockSpec((PEER_HEADS, 1, kc, t), lambda b, c: (0, cur(c), 0, b))
    gspec = pl.BlockSpec((PEER_HEADS, N_KEYS, t), lambda b, c: (0, 0, b))
    return pl.pallas_call(
        functools.partial(_peer_kernel, ec=ec, t=t),
        out_shape=jax.ShapeDtypeStruct((n, D_MODEL), F32),
        grid=(n // t, nc + 1),
        in_specs=[
            pl.BlockSpec((D_MODEL, t), lambda b, c: (0, b)),
            pl.BlockSpec((t, D_MODEL), lambda b, c: (b, 0)),
            pl.BlockSpec((ec, D_MODEL), lambda b, c: (cur(c), 0)),
            pl.BlockSpec((D_MODEL, ec), lambda b, c: (0, prv(c))),
            cspec, cspec, gspec, gspec,
            _const_spec((1, D_MODEL)),
        ],
        out_specs=pl.BlockSpec((t, D_MODEL), lambda b, c: (b, 0)),
        scratch_shapes=[
            pltpu.VMEM((D_MODEL, t), F32),
            pltpu.VMEM((ec, t), F32),
            pltpu.VMEM((2, ec, t), BF16),
        ],
        compiler_params=_params("parallel", "arbitrary"),
        name="peer",
    )(xnt, x2, u_bf16, vt_bf16, n1, e1, r2, e2, gf)


def _group(x_real, x_in, hist16, h0r, h0i, wts, *, pad_rows, pos0, t_in, t_seq, t_merge, t_tok):
    b, lp, _ = x_in.shape
    l = x_real.shape[1]
    proj = _inproj(x_in.reshape(b * lp, D_MODEL), wts["norm1_g"], wts["w_in"], t_in)
    proj3 = proj.reshape(b, lp, W_IN_COLS)
    out_a = _pool(proj3, hist16, wts["pool_w"], wts["pool_scale"], t_seq, pos0)
    out_b, hr, hi = _s5(proj3, h0r, h0i, wts["bblk"], wts["cblk"], wts["scan"], wts["ssm_d"], wts["w_glu"], t_seq)
    n = b * l
    if pad_rows == 0:
        x_real, out_a, out_b, projm = (a.reshape(1, n, a.shape[-1]) for a in (x_real, out_a, out_b, proj3))
    else:
        projm = proj3
    x2, xnt = _merge(x_real, out_a, out_b, projm, wts["proj_a"], wts["proj_b"], wts["w_out"],
                     wts["norm2_g"], t_merge, pad_rows // t_merge)
    scores = _query(xnt, wts["peer_wqt"], wts["peer_key1"], wts["peer_key2"], t_tok)
    n1, e1, r2, e2 = _gates(scores)
    y = _peer(xnt, x2.reshape(n, D_MODEL), wts["peer_u"], wts["peer_vt"], n1, e1, r2, e2,
              wts["final_norm_g"], t_tok, 512)
    new_hist = proj3[:, lp - POOL_HIST:, :W_POOL]
    shp = (b, N_SSM_GROUPS, SSM_STATE)
    return y.reshape(b, l, D_MODEL), new_hist, hr.reshape(shp), hi.reshape(shp)


PROMPT_PAD = 512


def kernel(x_prompt, x_sample, cache_pool, state_ssm_re, state_ssm_im, meta_tokens, norm1_g, w_in, pool_w, pool_scale, ssm_a_re, ssm_a_im, ssm_log_step, ssm_b_re, ssm_b_im, ssm_c_re, ssm_c_im, ssm_d, w_glu, proj_a, proj_b, w_out, norm2_g, peer_wq, peer_key1, peer_key2, peer_u, peer_v, final_norm_g):
    depth = norm1_g.shape[0]
    assert depth == 1
    bp, seq, _ = x_prompt.shape
    bs, dseq, _ = x_sample.shape
    past_len = 1024

    bblk, cblk, scan = _s5_weights(ssm_a_re[0], ssm_a_im[0], ssm_log_step[0], ssm_b_re[0], ssm_b_im[0],
                                   ssm_c_re[0], ssm_c_im[0])
    wts = {
        "norm1_g": norm1_g[0][None], "w_in": w_in[0].astype(BF16),
        "pool_w": pool_w[0].astype(BF16), "pool_scale": pool_scale[0][None],
        "bblk": bblk, "cblk": cblk, "scan": scan, "ssm_d": ssm_d[0][None], "w_glu": w_glu[0].astype(BF16),
        "proj_a": proj_a[0].astype(BF16), "proj_b": proj_b[0].astype(BF16), "w_out": w_out[0].astype(BF16),
        "norm2_g": norm2_g[0][None], "peer_wqt": peer_wq[0].astype(BF16).T,
        "peer_key1": peer_key1[0].astype(BF16), "peer_key2": peer_key2[0].astype(BF16),
        "peer_u": peer_u[0].astype(BF16), "peer_vt": peer_v[0].astype(BF16).T,
        "final_norm_g": final_norm_g[None],
    }

    zpad = jnp.zeros((bp, PROMPT_PAD - N_META, D_MODEL), x_prompt.dtype)
    meta = jnp.broadcast_to(meta_tokens[None], (bp, N_META, D_MODEL))
    xp = jnp.concatenate([zpad, meta, x_prompt], axis=1)
    zst = jnp.zeros((bp, SSM_SB, SSM_SB_LANES), F32)
    yp, pool_p, re_p, im_p = _group(
        x_prompt, xp, jnp.zeros((bp, POOL_HIST + 1, W_POOL), F32), zst, zst, wts,
        pad_rows=PROMPT_PAD, pos0=1 - (PROMPT_PAD - N_META), t_in=512, t_seq=256, t_merge=256, t_tok=512)

    hist16 = jnp.pad(cache_pool[0], ((0, 0), (1, 0), (0, 0)))
    ys, pool_s, re_s, im_s = _group(
        x_sample, x_sample, hist16,
        state_ssm_re[0].reshape(bs, SSM_SB, SSM_SB_LANES), state_ssm_im[0].reshape(bs, SSM_SB, SSM_SB_LANES),
        wts, pad_rows=0, pos0=1 + past_len, t_in=512, t_seq=dseq, t_merge=256, t_tok=512)

    return (yp, ys, pool_p[None], re_p[None], im_p[None], pool_s[None], re_s[None], im_s[None])
```

```python
import functools

import jax
import jax.numpy as jnp
from jax import lax
from jax.experimental import pallas as pl
from jax.experimental.pallas import tpu as pltpu

D_MODEL = 2048
N_META = 16
W_POOL = 1024
W_SSM = 1024
POOL_SIZES = (2, 4, 8, 16)
POOL_GROUP = 256
POOL_HIST = 15
N_SSM_GROUPS = 64
SSM_GROUP = 16
SSM_STATE = 64
PEER_HEADS = 8
PEER_TOPK = 16
N_KEYS = 128
N_EXPERTS = N_KEYS * N_KEYS
D_QUERY = 256
D_HALF = 128
W_IN_COLS = W_POOL + W_SSM + 2 * D_MODEL
NORM_EPS = 1e-6

F32 = jnp.float32
BF16 = jnp.bfloat16

SSM_SB = 4
SSM_SB_CH = W_SSM // SSM_SB
SSM_SB_LANES = (N_SSM_GROUPS // SSM_SB) * SSM_STATE
SCAN_ROWS = 8
SCAN_LANES = 256
SCAN_UNROLL = 4
CARRY_LANES = 512

VMEM_LIMIT = 56 * 1024 * 1024


def _const_spec(shape):
    nd = len(shape)
    return pl.BlockSpec(shape, lambda *_: (0,) * nd, pipeline_mode=pl.Buffered(1))


def _params(*sem, flags=None):
    return pltpu.CompilerParams(dimension_semantics=sem, vmem_limit_bytes=VMEM_LIMIT, flags=flags)


def _rms(x, g):
    return x * lax.rsqrt(jnp.mean(x * x, axis=-1, keepdims=True) + NORM_EPS) * g


def _inproj_kernel(x_ref, g_ref, w_ref, o_ref, xn_ref):
    @pl.when(pl.program_id(1) == 0)
    def _():
        xn_ref[...] = _rms(x_ref[...], g_ref[...]).astype(BF16)

    o_ref[...] = jnp.dot(xn_ref[...], w_ref[...], preferred_element_type=F32)


def _inproj(x, g, w_bf16, tm):
    n = x.shape[0]
    cb = 2048
    return pl.pallas_call(
        _inproj_kernel,
        out_shape=jax.ShapeDtypeStruct((n, W_IN_COLS), F32),
        grid=(n // tm, W_IN_COLS // cb),
        in_specs=[
            pl.BlockSpec((tm, D_MODEL), lambda i, j: (i, 0)),
            _const_spec((1, D_MODEL)),
            pl.BlockSpec((D_MODEL, cb), lambda i, j: (0, j)),
        ],
        out_specs=pl.BlockSpec((tm, cb), lambda i, j: (i, j)),
        scratch_shapes=[pltpu.VMEM((tm, D_MODEL), BF16)],
        compiler_params=_params("parallel", "arbitrary"),
        name="inproj",
    )(x, g, w_bf16)


def _pool_kernel(u_ref, hist_ref, pw_ref, ps_ref, o_ref, ext_ref, *, tp, pos0):
    i = pl.program_id(1)
    hrows = POOL_HIST + 1

    @pl.when(i == 0)
    def _():
        ext_ref[0:hrows, :] = hist_ref[0]

    @pl.when(i > 0)
    def _():
        ext_ref[0:hrows, :] = ext_ref[tp:tp + hrows, :]

    ext_ref[hrows:hrows + tp, :] = u_ref[0]
    row = lax.broadcasted_iota(jnp.int32, (tp, 1), 0)
    avail = i * tp + row + pos0
    for gi, w in enumerate(POOL_SIZES):
        c0 = gi * POOL_GROUP
        u = ext_ref[hrows:hrows + tp, c0:c0 + POOL_GROUP]
        acc = u
        for j in range(1, w):
            acc = acc + ext_ref[hrows - j:hrows - j + tp, c0:c0 + POOL_GROUP]
        cnt = jnp.maximum(jnp.minimum(avail, w), 1).astype(F32)
        pooled = acc / cnt - u
        mixed = jnp.dot(pooled.astype(BF16), pw_ref[gi], preferred_element_type=F32)
        o_ref[0, :, c0:c0 + POOL_GROUP] = (mixed * ps_ref[:, c0:c0 + POOL_GROUP]).astype(o_ref.dtype)


def _pool(proj3, hist16, pw_bf16, pscale, tp, pos0):
    b, l, _ = proj3.shape
    return pl.pallas_call(
        functools.partial(_pool_kernel, tp=tp, pos0=pos0),
        out_shape=jax.ShapeDtypeStruct((b, l, W_POOL), BF16),
        grid=(b, l // tp),
        in_specs=[
            pl.BlockSpec((1, tp, W_POOL), lambda s, i: (s, i, 0)),
            pl.BlockSpec((1, POOL_HIST + 1, W_POOL), lambda s, i: (s, 0, 0)),
            _const_spec((len(POOL_SIZES), POOL_GROUP, POOL_GROUP)),
            _const_spec((1, W_POOL)),
        ],
        out_specs=pl.BlockSpec((1, tp, W_POOL), lambda s, i: (s, i, 0)),
        scratch_shapes=[pltpu.VMEM((POOL_HIST + 1 + tp, W_POOL), F32)],
        compiler_params=_params("parallel", "arbitrary"),
        name="pool",
    )(proj3, hist16, pw_bf16, pscale)


def _s5_kernel(u_ref, h0r_ref, h0i_ref, bblk_ref, cblk_ref, sc_ref, d_ref, wglu_ref,
               o_ref, hr_out, hi_out, bu_ref, xs_ref, y_ref, cr_ref, ci_ref, *, ts):
    i = pl.program_id(1)

    @pl.when(i == 0)
    def _():
        cr_ref[...] = h0r_ref[0]
        ci_ref[...] = h0i_ref[0]

    u = u_ref[0]
    ub = u.astype(BF16)
    for sb in range(SSM_SB):
        bu_ref[...] = jnp.dot(ub[:, sb * SSM_SB_CH:(sb + 1) * SSM_SB_CH], bblk_ref[sb],
                              preferred_element_type=F32)
        for lc in range(SSM_SB_LANES // SCAN_LANES):
            re = slice(lc * SCAN_LANES, (lc + 1) * SCAN_LANES)
            im = slice(SSM_SB_LANES + lc * SCAN_LANES, SSM_SB_LANES + (lc + 1) * SCAN_LANES)
            a1r, a1i, a2r, a2i, a4r, a4i = [sc_ref[sb, k, :, re] for k in range(6)]
            steps = ((a1r, a1i, 1), (a2r, a2i, 2), (a4r, a4i, 4))

            def local_scan(r, _, re=re, im=im, steps=steps):
                rows = pl.ds(pl.multiple_of(r * SCAN_ROWS, SCAN_ROWS), SCAN_ROWS)
                xr = bu_ref[rows, re]
                xi = bu_ref[rows, im]
                for ar, ai, d in steps:
                    sr = pltpu.roll(xr, d, 0)
                    si = pltpu.roll(xi, d, 0)
                    xr, xi = xr + ar * sr - ai * si, xi + ar * si + ai * sr
                xs_ref[rows, re] = xr
                xs_ref[rows, im] = xi
                return 0

            lax.fori_loop(0, ts // SCAN_ROWS, local_scan, 0, unroll=SCAN_UNROLL)
        for lc in range(SSM_SB_LANES // CARRY_LANES):
            re = slice(lc * CARRY_LANES, (lc + 1) * CARRY_LANES)
            im = slice(SSM_SB_LANES + lc * CARRY_LANES, SSM_SB_LANES + (lc + 1) * CARRY_LANES)
            pr = sc_ref[sb, 6, :, re]
            pi = sc_ref[sb, 7, :, re]
            cr0 = jnp.broadcast_to(cr_ref[sb:sb + 1, re], (SCAN_ROWS, CARRY_LANES))
            ci0 = jnp.broadcast_to(ci_ref[sb:sb + 1, re], (SCAN_ROWS, CARRY_LANES))

            def add_carry(r, carry, re=re, im=im, pr=pr, pi=pi):
                cr, ci = carry
                rows = pl.ds(pl.multiple_of(r * SCAN_ROWS, SCAN_ROWS), SCAN_ROWS)
                hr = xs_ref[rows, re] + pr * cr - pi * ci
                hi = xs_ref[rows, im] + pr * ci + pi * cr
                bu_ref[rows, re] = hr
                bu_ref[rows, im] = hi
                last = slice(SCAN_ROWS - 1, SCAN_ROWS)
                return (jnp.broadcast_to(hr[last], (SCAN_ROWS, CARRY_LANES)),
                        jnp.broadcast_to(hi[last], (SCAN_ROWS, CARRY_LANES)))

            cr, ci = lax.fori_loop(0, ts // SCAN_ROWS, add_carry, (cr0, ci0), unroll=2)
            cr_ref[sb:sb + 1, re] = cr[0:1]
            ci_ref[sb:sb + 1, re] = ci[0:1]
        y_ref[:, sb * SSM_SB_CH:(sb + 1) * SSM_SB_CH] = jnp.dot(
            bu_ref[...].astype(BF16), cblk_ref[sb], preferred_element_type=F32)
    y = jax.nn.gelu(y_ref[...] + d_ref[...] * u)
    z = jnp.dot(y.astype(BF16), wglu_ref[...], preferred_element_type=F32)
    o_ref[0] = (y * jax.nn.sigmoid(z)).astype(o_ref.dtype)

    @pl.when(i == pl.num_programs(1) - 1)
    def _():
        hr_out[0] = cr_ref[...]
        hi_out[0] = ci_ref[...]


def _s5(proj3, h0r, h0i, bblk, cblk, scan_consts, d_skip, wglu_bf16, ts):
    b, l, _ = proj3.shape
    st_spec = pl.BlockSpec((1, SSM_SB, SSM_SB_LANES), lambda s, i: (s, 0, 0))
    st_shape = jax.ShapeDtypeStruct((b, SSM_SB, SSM_SB_LANES), F32)
    return pl.pallas_call(
        functools.partial(_s5_kernel, ts=ts),
        out_shape=(jax.ShapeDtypeStruct((b, l, W_SSM), BF16), st_shape, st_shape),
        grid=(b, l // ts),
        in_specs=[
            pl.BlockSpec((1, ts, W_SSM), lambda s, i: (s, i, 1)),
            st_spec, st_spec,
            _const_spec(bblk.shape), _const_spec(cblk.shape), _const_spec(scan_consts.shape),
            _const_spec((1, W_SSM)), _const_spec((W_SSM, W_SSM)),
        ],
        out_specs=(pl.BlockSpec((1, ts, W_SSM), lambda s, i: (s, i, 0)), st_spec, st_spec),
        scratch_shapes=[
            pltpu.VMEM((ts, 2 * SSM_SB_LANES), F32),
            pltpu.VMEM((ts, 2 * SSM_SB_LANES), F32),
            pltpu.VMEM((ts, W_SSM), F32),
            pltpu.VMEM((SSM_SB, SSM_SB_LANES), F32),
            pltpu.VMEM((SSM_SB, SSM_SB_LANES), F32),
        ],
        compiler_params=_params("parallel", "arbitrary"),
        name="s5",
    )(proj3, h0r, h0i, bblk, cblk, scan_consts, d_skip, wglu_bf16)


def _s5_weights(a_re, a_im, log_step, b_re, b_im, c_re, c_im):
    dt = jnp.exp(log_step)[:, None]
    mag = jnp.exp(a_re * dt)
    abr = mag * jnp.cos(a_im * dt)
    abi = mag * jnp.sin(a_im * dt)
    nr, ni = abr - 1.0, abi
    den = a_re * a_re + a_im * a_im
    f_re = (nr * a_re + ni * a_im) / den
    f_im = (ni * a_re - nr * a_im) / den
    bbr = f_re[..., None] * b_re - f_im[..., None] * b_im
    bbi = f_re[..., None] * b_im + f_im[..., None] * b_re
    gl = N_SSM_GROUPS // SSM_SB
    eye = jnp.eye(gl, dtype=F32)

    def bdiag_in(x):
        x = x.reshape(SSM_SB, gl, SSM_STATE, SSM_GROUP)
        return jnp.einsum("sgpc,gh->sgchp", x, eye).reshape(SSM_SB, gl * SSM_GROUP, gl * SSM_STATE)

    def bdiag_out(x):
        x = x.reshape(SSM_SB, gl, SSM_GROUP, SSM_STATE)
        return jnp.einsum("sgcp,gh->sgphc", x, eye).reshape(SSM_SB, gl * SSM_STATE, gl * SSM_GROUP)

    bblk = jnp.concatenate([bdiag_in(bbr), bdiag_in(bbi)], axis=2).astype(BF16)
    cblk = jnp.concatenate([bdiag_out(c_re), -bdiag_out(c_im)], axis=1).astype(BF16)

    ar = abr.reshape(SSM_SB, SSM_SB_LANES)
    ai = abi.reshape(SSM_SB, SSM_SB_LANES)
    pw_r, pw_i = [ar], [ai]
    for _ in range(SCAN_ROWS - 1):
        pr, pi = pw_r[-1], pw_i[-1]
        pw_r.append(pr * ar - pi * ai)
        pw_i.append(pr * ai + pi * ar)
    row = jnp.arange(SCAN_ROWS)[None, :, None]

    def step(d):
        m = (row >= d).astype(F32)
        return pw_r[d - 1][:, None, :] * m, pw_i[d - 1][:, None, :] * m

    s1r, s1i = step(1)
    s2r, s2i = step(2)
    s4r, s4i = step(4)
    car_r = jnp.stack(pw_r, axis=1)
    car_i = jnp.stack(pw_i, axis=1)
    consts = jnp.stack([s1r, s1i, s2r, s2i, s4r, s4i, car_r, car_i], axis=1)
    return bblk, cblk, consts


def _merge_kernel(x_ref, oa_ref, ob_ref, ga_ref, gb_ref, pa_ref, pb_ref, wo_ref, g2_ref,
                  x2_ref, xnt_ref):
    ma = jnp.dot(oa_ref[0], pa_ref[...], preferred_element_type=F32)
    mb = jnp.dot(ob_ref[0], pb_ref[...], preferred_element_type=F32)
    merged = jax.nn.sigmoid(ga_ref[0]) * ma + jax.nn.sigmoid(gb_ref[0]) * mb
    x2 = x_ref[0] + jnp.dot(merged.astype(BF16), wo_ref[...], preferred_element_type=F32)
    x2_ref[0] = x2
    xnt_ref[...] = _rms(x2, g2_ref[...]).T.astype(BF16)


def _merge(x3, oa3, ob3, proj3, pa, pb, wo, g2, tm, pad_blocks):
    b, l, _ = x3.shape
    off = pad_blocks
    nblk = l // tm
    return pl.pallas_call(
        _merge_kernel,
        out_shape=(jax.ShapeDtypeStruct((b, l, D_MODEL), F32), jax.ShapeDtypeStruct((D_MODEL, b * l), BF16)),
        grid=(b, nblk),
        in_specs=[
            pl.BlockSpec((1, tm, D_MODEL), lambda s, i: (s, i, 0)),
            pl.BlockSpec((1, tm, W_POOL), lambda s, i: (s, i + off, 0)),
            pl.BlockSpec((1, tm, W_SSM), lambda s, i: (s, i + off, 0)),
            pl.BlockSpec((1, tm, D_MODEL), lambda s, i: (s, i + off, 1)),
            pl.BlockSpec((1, tm, D_MODEL), lambda s, i: (s, i + off, 2)),
            _const_spec((W_POOL, D_MODEL)), _const_spec((W_SSM, D_MODEL)),
            _const_spec((D_MODEL, D_MODEL)), _const_spec((1, D_MODEL)),
        ],
        out_specs=(pl.BlockSpec((1, tm, D_MODEL), lambda s, i: (s, i, 0)),
                   pl.BlockSpec((D_MODEL, tm), lambda s, i: (0, s * nblk + i))),
        compiler_params=_params("parallel", "parallel"),
        name="merge",
    )(x3, oa3, ob3, proj3, proj3, pa, pb, wo, g2)


def _query_kernel(xnt_ref, wqt_ref, k1_ref, k2_ref, s_ref):
    qt = jnp.dot(wqt_ref[...], xnt_ref[...], preferred_element_type=F32).astype(BF16)
    for h in range(PEER_HEADS):
        r0 = h * D_QUERY
        s_ref[h, 0] = jnp.dot(k1_ref[...], qt[r0:r0 + D_HALF], preferred_element_type=F32)
        s_ref[h, 1] = jnp.dot(k2_ref[...], qt[r0 + D_HALF:r0 + D_QUERY], preferred_element_type=F32)


def _query(xnt, wqt, k1, k2, tm):
    n = xnt.shape[1]
    return pl.pallas_call(
        _query_kernel,
        out_shape=jax.ShapeDtypeStruct((PEER_HEADS, 2, N_KEYS, n), F32),
        grid=(n // tm,),
        in_specs=[
            pl.BlockSpec((D_MODEL, tm), lambda i: (0, i)),
            _const_spec((PEER_HEADS * D_QUERY, D_MODEL)),
            _const_spec((N_KEYS, D_HALF)), _const_spec((N_KEYS, D_HALF)),
        ],
        out_specs=pl.BlockSpec((PEER_HEADS, 2, N_KEYS, tm), lambda i: (0, 0, 0, i)),
        compiler_params=_params("parallel"),
        name="query",
    )(xnt, wqt, k1, k2)


LANE = 128
SUBLANE = 8
PACK_ROWS = 16
GATE_TOKENS = SUBLANE * LANE


def _oddeven_merge_sort_pairs(n):
    pairs = []
    p = 1
    while p < n:
        k = p
        while k >= 1:
            for j in range(k % p, n - k, 2 * k):
                for i in range(min(k, n - j - k)):
                    if (i + j) // (p * 2) == (i + j + k) // (p * 2):
                        pairs.append((i + j, i + j + k))
            k //= 2
        p *= 2
    return pairs


_SORT16_PAIRS = _oddeven_merge_sort_pairs(PEER_TOPK)


def _cmpx(v, i, j):
    a, b = v[i], v[j]
    v[i] = jnp.maximum(a, b)
    v[j] = jnp.minimum(a, b)


def _sort16(v):
    v = list(v)
    for i, j in _SORT16_PAIRS:
        _cmpx(v, i, j)
    return v


def _merge_top16(a, b):
    k = PEER_TOPK
    c = [a[i] if b[k - 1 - i] is None else jnp.maximum(a[i], b[k - 1 - i]) for i in range(k)]
    d = k // 2
    while d >= 1:
        for i in range(k):
            if not i & d:
                _cmpx(c, i, i + d)
        d //= 2
    return c


def _top16_values(rows):
    groups = [_sort16(rows[g:g + PEER_TOPK]) for g in range(0, len(rows), PEER_TOPK)]
    while len(groups) > 1:
        groups = [_merge_top16(groups[g], groups[g + 1]) for g in range(0, len(groups), 2)]
    return groups[0]


def _top16(val):
    rows = val.shape[0]
    rowid = lax.broadcasted_iota(jnp.int32, val.shape, 0).astype(F32)
    kshape = (PEER_TOPK,) + val.shape[1:]
    kid = lax.broadcasted_iota(jnp.int32, kshape, 0)

    def body(r, carry):
        val, rank, tops = carry
        m = jnp.max(val, axis=0, keepdims=True)
        idx = jnp.min(jnp.where(val == m, rowid, float(rows)), axis=0, keepdims=True)
        hit = rowid == idx
        rank = jnp.where(hit, lax.convert_element_type(r, F32), rank)
        val = jnp.where(hit, -jnp.inf, val)
        tops = jnp.where(kid == r, m, tops)
        return val, rank, tops

    init = (val, jnp.full(val.shape, float(PEER_TOPK), F32), jnp.zeros(kshape, F32))
    _, rank, tops = lax.fori_loop(0, PEER_TOPK, body, init)
    return rank, tops


def _outer16(a, b, op):
    return jnp.concatenate([op(a[r:r + 1], b) for r in range(PEER_TOPK)], axis=0)


def _gates_kernel(s_ref, n1_ref, e1_ref, r2_ref, e2_ref):
    k = PEER_TOPK
    s1 = [s_ref[0, 0, i] for i in range(N_KEYS)]
    s2 = [s_ref[0, 1, i] for i in range(N_KEYS)]
    v1 = _top16_values(s1)
    v2 = _top16_values(s2)
    width = [k // (r + 1) for r in range(k)]
    cell = [[v1[r1] + v2[r2] for r2 in range(width[r1])] for r1 in range(k)]
    top = cell[0]
    for r1 in range(1, k):
        top = _merge_top16(top, cell[r1] + [None] * (k - width[r1]))
    tau = top[k - 1]

    def count(vals, thr, cmp):
        n = jnp.zeros_like(thr)
        for v in vals:
            n = n + cmp(v, thr).astype(F32)
        return n

    x1 = [jnp.exp(v - v1[0]) for v in v1]
    x2 = [jnp.exp(v - v2[0]) for v in v2]
    z = jnp.zeros_like(tau)
    nrow = []
    for r1 in range(k):
        n = jnp.zeros_like(tau)
        for r2 in range(width[r1]):
            hit = cell[r1][r2] >= tau
            n = n + hit.astype(F32)
            z = z + jnp.where(hit, x1[r1] * x2[r2], 0.0)
        nrow.append(n)
    ncell = nrow[0]
    for n in nrow[1:]:
        ncell = ncell + n
    ge = lambda a, b: a >= b
    ambiguous = ((count(s1, v1[k - 1], ge) != float(k)) | (count(s2, v2[k - 1], ge) != float(k))
                 | (ncell != float(k)))
    for r1 in range(1, k):
        ambiguous = ambiguous | (v1[r1] + v2[width[r1]] >= tau)
    col = []
    for j in range(1, k + 1):
        c = jnp.full_like(tau, jnp.inf)
        for r1 in range(k // j):
            c = jnp.minimum(c, jnp.where(nrow[r1] >= float(j), v1[r1], jnp.inf))
        col.append(c)
    inv_z = 1.0 / z
    gt = lambda a, b: a > b
    for i in range(N_KEYS):
        n1_ref[0, i] = count(col, s1[i], lambda c, s: s >= c)
        r2_ref[0, i] = count(v2, s2[i], gt)
        e1_ref[0, i] = jnp.where(s1[i] >= v1[k - 1], jnp.exp(s1[i] - v1[0]) * inv_z, 0.0)
        e2_ref[0, i] = jnp.where(s2[i] >= v2[k - 1], jnp.exp(s2[i] - v2[0]), 0.0)

    @pl.when(jnp.max(ambiguous.astype(F32)) > 0.0)
    def _():
        t1 = s_ref[0, 0]
        t2 = s_ref[0, 1]
        rank1, w1 = _top16(t1)
        rank2, w2 = _top16(t2)
        crank, _ = _top16(_outer16(w1, w2, jnp.add))
        sel = (crank < float(k)).astype(F32)
        y1 = jnp.exp(w1 - w1[0:1])
        y2 = jnp.exp(w2 - w2[0:1])
        zz = jnp.sum(sel * _outer16(y1, y2, jnp.multiply), axis=0, keepdims=True)
        n1 = jnp.zeros_like(t1)
        for r in range(k):
            cnt = jnp.sum(sel[r * k:(r + 1) * k], axis=0, keepdims=True)
            n1 = jnp.where(rank1 == float(r), cnt, n1)
        n1_ref[0] = n1
        r2_ref[0] = rank2
        e1_ref[0] = jnp.exp(t1 - w1[0:1]) / zz
        e2_ref[0] = jnp.exp(t2 - w2[0:1])


def _gates(scores):
    n = scores.shape[-1]
    nt = n // LANE
    rows = SUBLANE
    s5d = scores.reshape(PEER_HEADS, 2, N_KEYS, nt, LANE)
    spec = pl.BlockSpec((1, N_KEYS, rows, LANE), lambda i, h: (h, 0, i, 0))
    shp = jax.ShapeDtypeStruct((PEER_HEADS, N_KEYS, nt, LANE), F32)
    outs = pl.pallas_call(
        _gates_kernel,
        out_shape=(shp, shp, shp, shp),
        grid=(nt // rows, PEER_HEADS),
        in_specs=[pl.BlockSpec((1, 2, N_KEYS, rows, LANE), lambda i, h: (h, 0, 0, i, 0))],
        out_specs=(spec, spec, spec, spec),
        compiler_params=_params("parallel", "parallel"),
        name="gates",
    )(s5d)
    return tuple(a.reshape(PEER_HEADS, N_KEYS, n) for a in outs)


def _peer_kernel(xnt_ref, x2_ref, u_ref, vt_ref, n1_ref, e1_ref, r2_ref, e2_ref, gf_ref,
                 o_ref, acc_ref, w_ref, coef_ref, *, ec, t):
    c = pl.program_id(1)
    slot = c % 2

    @pl.when(c == 0)
    def _():
        coef_ref[1] = jnp.zeros((ec, t), BF16)

    kt = 256
    for lt in range(t // LANE):
        lanes = slice(lt * LANE, (lt + 1) * LANE)
        for j in range(ec // N_KEYS):
            n1b = [jnp.broadcast_to(n1_ref[h, 0, j:j + 1, lanes], (SUBLANE, LANE)) for h in range(PEER_HEADS)]
            e1b = [jnp.broadcast_to(e1_ref[h, 0, j:j + 1, lanes], (SUBLANE, LANE)) for h in range(PEER_HEADS)]
            for ib in range(N_KEYS // SUBLANE):
                rows = slice(ib * SUBLANE, (ib + 1) * SUBLANE)
                w = jnp.zeros((SUBLANE, LANE), F32)
                for h in range(PEER_HEADS):
                    hit = r2_ref[h, rows, lanes] < n1b[h]
                    w = w + jnp.where(hit, e1b[h] * e2_ref[h, rows, lanes], 0.0)
                w_ref[j * N_KEYS + ib * SUBLANE:j * N_KEYS + (ib + 1) * SUBLANE, lanes] = w
    prev = jnp.where(c == 0, 0.0, acc_ref[...])
    acc_ref[...] = prev + jnp.dot(vt_ref[:, 0:kt], coef_ref[1 - slot, 0:kt, :], preferred_element_type=F32)
    for k0 in range(kt, ec, kt):
        acc_ref[...] += jnp.dot(vt_ref[:, k0:k0 + kt], coef_ref[1 - slot, k0:k0 + kt, :],
                                preferred_element_type=F32)
    act = jax.nn.gelu(jnp.dot(u_ref[...], xnt_ref[...], preferred_element_type=F32))
    coef_ref[slot] = (w_ref[...] * act).astype(BF16)

    @pl.when(c == pl.num_programs(1) - 1)
    def _():
        o_ref[...] = _rms(x2_ref[...] + acc_ref[...].T, gf_ref[...])


def _peer(xnt, x2, u_bf16, vt_bf16, n1, e1, r2, e2, gf, t, ec):
    n = xnt.shape[1]
    nc = N_EXPERTS // ec
    kc = ec // N_KEYS
    n1 = n1.reshape(PEER_HEADS, nc, kc, n)
    e1 = e1.reshape(PEER_HEADS, nc, kc, n)
    cur = lambda c: jnp.minimum(c, nc - 1)
    prv = lambda c: jnp.maximum(c - 1, 0)
    cspec = pl.BlockSpec((PEER_HEADS, 1, kc, t), lambda b, c: (0, cur(c), 0, b))
    gspec = pl.BlockSpec((PEER_HEADS, N_KEYS, t), lambda b, c: (0, 0, b))
    return pl.pallas_call(
        functools.partial(_peer_kernel, ec=ec, t=t),
        out_shape=jax.ShapeDtypeStruct((n, D_MODEL), F32),
        grid=(n // t, nc + 1),
        in_specs=[
            pl.BlockSpec((D_MODEL, t), lambda b, c: (0, b)),
            pl.BlockSpec((t, D_MODEL), lambda b, c: (b, 0)),
            pl.BlockSpec((ec, D_MODEL), lambda b, c: (cur(c), 0)),
            pl.BlockSpec((D_MODEL, ec), lambda b, c: (0, prv(c))),
            cspec, cspec, gspec, gspec,
            _const_spec((1, D_MODEL)),
        ],
        out_specs=pl.BlockSpec((t, D_MODEL), lambda b, c: (b, 0)),
        scratch_shapes=[
            pltpu.VMEM((D_MODEL, t), F32),
            pltpu.VMEM((ec, t), F32),
            pltpu.VMEM((2, ec, t), BF16),
        ],
        compiler_params=_params("parallel", "arbitrary"),
        name="peer",
    )(xnt, x2, u_bf16, vt_bf16, n1, e1, r2, e2, gf)


def _group(x_real, x_in, hist16, h0r, h0i, wts, *, pad_rows, pos0, t_in, t_seq, t_merge, t_tok):
    b, lp, _ = x_in.shape
    l = x_real.shape[1]
    proj = _inproj(x_in.reshape(b * lp, D_MODEL), wts["norm1_g"], wts["w_in"], t_in)
    proj3 = proj.reshape(b, lp, W_IN_COLS)
    out_a = _pool(proj3, hist16, wts["pool_w"], wts["pool_scale"], t_seq, pos0)
    out_b, hr, hi = _s5(proj3, h0r, h0i, wts["bblk"], wts["cblk"], wts["scan"], wts["ssm_d"], wts["w_glu"], t_seq)
    n = b * l
    if pad_rows == 0:
        x_real, out_a, out_b, projm = (a.reshape(1, n, a.shape[-1]) for a in (x_real, out_a, out_b, proj3))
    else:
        projm = proj3
    x2, xnt = _merge(x_real, out_a, out_b, projm, wts["proj_a"], wts["proj_b"], wts["w_out"],
                     wts["norm2_g"], t_merge, pad_rows // t_merge)
    scores = _query(xnt, wts["peer_wqt"], wts["peer_key1"], wts["peer_key2"], t_tok)
    n1, e1, r2, e2 = _gates(scores)
    y = _peer(xnt, x2.reshape(n, D_MODEL), wts["peer_u"], wts["peer_vt"], n1, e1, r2, e2,
              wts["final_norm_g"], t_tok, 512)
    new_hist = proj3[:, lp - POOL_HIST:, :W_POOL]
    shp = (b, N_SSM_GROUPS, SSM_STATE)
    return y.reshape(b, l, D_MODEL), new_hist, hr.reshape(shp), hi.reshape(shp)


PROMPT_PAD = 512


def kernel(x_prompt, x_sample, cache_pool, state_ssm_re, state_ssm_im, meta_tokens, norm1_g, w_in, pool_w, pool_scale, ssm_a_re, ssm_a_im, ssm_log_step, ssm_b_re, ssm_b_im, ssm_c_re, ssm_c_im, ssm_d, w_glu, proj_a, proj_b, w_out, norm2_g, peer_wq, peer_key1, peer_key2, peer_u, peer_v, final_norm_g):
    depth = norm1_g.shape[0]
    assert depth == 1
    bp, seq, _ = x_prompt.shape
    bs, dseq, _ = x_sample.shape
    past_len = 1024

    bblk, cblk, scan = _s5_weights(ssm_a_re[0], ssm_a_im[0], ssm_log_step[0], ssm_b_re[0], ssm_b_im[0],
                                   ssm_c_re[0], ssm_c_im[0])
    wts = {
        "norm1_g": norm1_g[0][None], "w_in": w_in[0].astype(BF16),
        "pool_w": pool_w[0].astype(BF16), "pool_scale": pool_scale[0][None],
        "bblk": bblk, "cblk": cblk, "scan": scan, "ssm_d": ssm_d[0][None], "w_glu": w_glu[0].astype(BF16),
        "proj_a": proj_a[0].astype(BF16), "proj_b": proj_b[0].astype(BF16), "w_out": w_out[0].astype(BF16),
        "norm2_g": norm2_g[0][None], "peer_wqt": peer_wq[0].astype(BF16).T,
        "peer_key1": peer_key1[0].astype(BF16), "peer_key2": peer_key2[0].astype(BF16),
        "peer_u": peer_u[0].astype(BF16), "peer_vt": peer_v[0].astype(BF16).T,
        "final_norm_g": final_norm_g[None],
    }

    zpad = jnp.zeros((bp, PROMPT_PAD - N_META, D_MODEL), x_prompt.dtype)
    meta = jnp.broadcast_to(meta_tokens[None], (bp, N_META, D_MODEL))
    xp = jnp.concatenate([zpad, meta, x_prompt], axis=1)
    zst = jnp.zeros((bp, SSM_SB, SSM_SB_LANES), F32)
    yp, pool_p, re_p, im_p = _group(
        x_prompt, xp, jnp.zeros((bp, POOL_HIST + 1, W_POOL), F32), zst, zst, wts,
        pad_rows=PROMPT_PAD, pos0=1 - (PROMPT_PAD - N_META), t_in=512, t_seq=256, t_merge=256, t_tok=512)

    hist16 = jnp.pad(cache_pool[0], ((0, 0), (1, 0), (0, 0)))
    ys, pool_s, re_s, im_s = _group(
        x_sample, x_sample, hist16,
        state_ssm_re[0].reshape(bs, SSM_SB, SSM_SB_LANES), state_ssm_im[0].reshape(bs, SSM_SB, SSM_SB_LANES),
        wts, pad_rows=0, pos0=1 + past_len, t_in=512, t_seq=dseq, t_merge=256, t_tok=512)

    return (yp, ys, pool_p[None], re_p[None], im_p[None], pool_s[None], re_s[None], im_s[None])
```

```python
import functools

import jax
import jax.numpy as jnp
from jax import lax
from jax.experimental import pallas as pl
from jax.experimental.pallas import tpu as pltpu

D_MODEL = 2048
N_META = 16
W_POOL = 1024
W_SSM = 1024
POOL_SIZES = (2, 4, 8, 16)
POOL_GROUP = 256
POOL_HIST = 15
N_SSM_GROUPS = 64
SSM_GROUP = 16
SSM_STATE = 64
PEER_HEADS = 8
PEER_TOPK = 16
N_KEYS = 128
N_EXPERTS = N_KEYS * N_KEYS
D_QUERY = 256
D_HALF = 128
W_IN_COLS = W_POOL + W_SSM + 2 * D_MODEL
NORM_EPS = 1e-6

F32 = jnp.float32
BF16 = jnp.bfloat16

SSM_SB = 4
SSM_SB_CH = W_SSM // SSM_SB
SSM_SB_LANES = (N_SSM_GROUPS // SSM_SB) * SSM_STATE
SCAN_ROWS = 8
SCAN_LANES = 256
SCAN_UNROLL = 4
CARRY_LANES = 512

VMEM_LIMIT = 56 * 1024 * 1024


def _const_spec(shape):
    nd = len(shape)
    return pl.BlockSpec(shape, lambda *_: (0,) * nd, pipeline_mode=pl.Buffered(1))


def _params(*sem, flags=None):
    return pltpu.CompilerParams(dimension_semantics=sem, vmem_limit_bytes=VMEM_LIMIT, flags=flags)


def _rms(x, g):
    return x * lax.rsqrt(jnp.mean(x * x, axis=-1, keepdims=True) + NORM_EPS) * g


def _inproj_kernel(x_ref, f_ref, g_ref, w_ref, o_ref, xn_ref, *, front_blocks):
    i = pl.program_id(1)
    first_col = pl.program_id(2) == 0

    @pl.when(first_col & (i >= front_blocks))
    def _():
        xn_ref[...] = _rms(x_ref[0], g_ref[...]).astype(BF16)

    if front_blocks:
        @pl.when(first_col & (i < front_blocks))
        def _():
            xn_ref[...] = _rms(f_ref[...], g_ref[...]).astype(BF16)

    o_ref[0] = jnp.dot(xn_ref[...], w_ref[...], preferred_element_type=F32)


def _inproj(x3, front, g, w_bf16, tm):
    b, l, _ = x3.shape
    cb = 2048
    fb = 0 if front is None else 1
    if front is None:
        front = jnp.zeros((SUBLANE, D_MODEL), F32)
    return pl.pallas_call(
        functools.partial(_inproj_kernel, front_blocks=fb),
        out_shape=jax.ShapeDtypeStruct((b, fb * tm + l, W_IN_COLS), F32),
        grid=(b, fb + l // tm, W_IN_COLS // cb),
        in_specs=[
            pl.BlockSpec((1, tm, D_MODEL), lambda s, i, j: (s, jnp.maximum(i - fb, 0), 0)),
            _const_spec(front.shape),
            _const_spec((1, D_MODEL)),
            pl.BlockSpec((D_MODEL, cb), lambda s, i, j: (0, j)),
        ],
        out_specs=pl.BlockSpec((1, tm, cb), lambda s, i, j: (s, i, j)),
        scratch_shapes=[pltpu.VMEM((tm, D_MODEL), BF16)],
        compiler_params=_params("parallel", "arbitrary", "arbitrary"),
        name="inproj",
    )(x3, front, g, w_bf16)


def _pool_kernel(u_ref, hist_ref, pw_ref, ps_ref, o_ref, ext_ref, *, tp, pos0):
    i = pl.program_id(1)
    hrows = POOL_HIST + 1

    @pl.when(i == 0)
    def _():
        ext_ref[0:hrows, :] = hist_ref[0]

    @pl.when(i > 0)
    def _():
        ext_ref[0:hrows, :] = ext_ref[tp:tp + hrows, :]

    ext_ref[hrows:hrows + tp, :] = u_ref[0]
    row = lax.broadcasted_iota(jnp.int32, (tp, 1), 0)
    avail = i * tp + row + pos0
    for gi, w in enumerate(POOL_SIZES):
        c0 = gi * POOL_GROUP
        u = ext_ref[hrows:hrows + tp, c0:c0 + POOL_GROUP]
        acc = u
        for j in range(1, w):
            acc = acc + ext_ref[hrows - j:hrows - j + tp, c0:c0 + POOL_GROUP]
        cnt = jnp.maximum(jnp.minimum(avail, w), 1).astype(F32)
        pooled = acc / cnt - u
        mixed = jnp.dot(pooled.astype(BF16), pw_ref[gi], preferred_element_type=F32)
        o_ref[0, :, c0:c0 + POOL_GROUP] = (mixed * ps_ref[:, c0:c0 + POOL_GROUP]).astype(o_ref.dtype)


def _pool(proj3, hist16, pw_bf16, pscale, tp, pos0):
    b, l, _ = proj3.shape
    return pl.pallas_call(
        functools.partial(_pool_kernel, tp=tp, pos0=pos0),
        out_shape=jax.ShapeDtypeStruct((b, l, W_POOL), BF16),
        grid=(b, l // tp),
        in_specs=[
            pl.BlockSpec((1, tp, W_POOL), lambda s, i: (s, i, 0)),
            pl.BlockSpec((1, POOL_HIST + 1, W_POOL), lambda s, i: (s, 0, 0)),
            _const_spec((len(POOL_SIZES), POOL_GROUP, POOL_GROUP)),
            _const_spec((1, W_POOL)),
        ],
        out_specs=pl.BlockSpec((1, tp, W_POOL), lambda s, i: (s, i, 0)),
        scratch_shapes=[pltpu.VMEM((POOL_HIST + 1 + tp, W_POOL), F32)],
        compiler_params=_params("parallel", "arbitrary"),
        name="pool",
    )(proj3, hist16, pw_bf16, pscale)


def _s5_kernel(u_ref, h0r_ref, h0i_ref, bblk_ref, cblk_ref, sc_ref, d_ref, wglu_ref,
               o_ref, hr_out, hi_out, bu_ref, xs_ref, y_ref, cr_ref, ci_ref, *, ts):
    i = pl.program_id(1)

    @pl.when(i == 0)
    def _():
        cr_ref[...] = h0r_ref[0]
        ci_ref[...] = h0i_ref[0]

    u = u_ref[0]
    ub = u.astype(BF16)
    for sb in range(SSM_SB):
        bu_ref[...] = jnp.dot(ub[:, sb * SSM_SB_CH:(sb + 1) * SSM_SB_CH], bblk_ref[sb],
                              preferred_element_type=F32)
        for lc in range(SSM_SB_LANES // SCAN_LANES):
            re = slice(lc * SCAN_LANES, (lc + 1) * SCAN_LANES)
            im = slice(SSM_SB_LANES + lc * SCAN_LANES, SSM_SB_LANES + (lc + 1) * SCAN_LANES)
            a1r, a1i, a2r, a2i, a4r, a4i = [sc_ref[sb, k, :, re] for k in range(6)]
            steps = ((a1r, a1i, 1), (a2r, a2i, 2), (a4r, a4i, 4))

            def local_scan(r, _, re=re, im=im, steps=steps):
                rows = pl.ds(pl.multiple_of(r * SCAN_ROWS, SCAN_ROWS), SCAN_ROWS)
                xr = bu_ref[rows, re]
                xi = bu_ref[rows, im]
                for ar, ai, d in steps:
                    sr = pltpu.roll(xr, d, 0)
                    si = pltpu.roll(xi, d, 0)
                    xr, xi = xr + ar * sr - ai * si, xi + ar * si + ai * sr
                xs_ref[rows, re] = xr
                xs_ref[rows, im] = xi
                return 0

            lax.fori_loop(0, ts // SCAN_ROWS, local_scan, 0, unroll=SCAN_UNROLL)
        for lc in range(SSM_SB_LANES // CARRY_LANES):
            re = slice(lc * CARRY_LANES, (lc + 1) * CARRY_LANES)
            im = slice(SSM_SB_LANES + lc * CARRY_LANES, SSM_SB_LANES + (lc + 1) * CARRY_LANES)
            pr = sc_ref[sb, 6, :, re]
            pi = sc_ref[sb, 7, :, re]
            cr0 = jnp.broadcast_to(cr_ref[sb:sb + 1, re], (SCAN_ROWS, CARRY_LANES))
            ci0 = jnp.broadcast_to(ci_ref[sb:sb + 1, re], (SCAN_ROWS, CARRY_LANES))

            def add_carry(r, carry, re=re, im=im, pr=pr, pi=pi):
                cr, ci = carry
                rows = pl.ds(pl.multiple_of(r * SCAN_ROWS, SCAN_ROWS), SCAN_ROWS)
                hr = xs_ref[rows, re] + pr * cr - pi * ci
                hi = xs_ref[rows, im] + pr * ci + pi * cr
                bu_ref[rows, re] = hr
                bu_ref[rows, im] = hi
                last = slice(SCAN_ROWS - 1, SCAN_ROWS)
                return (jnp.broadcast_to(hr[last], (SCAN_ROWS, CARRY_LANES)),
                        jnp.broadcast_to(hi[last], (SCAN_ROWS, CARRY_LANES)))

            cr, ci = lax.fori_loop(0, ts // SCAN_ROWS, add_carry, (cr0, ci0), unroll=2)
            cr_ref[sb:sb + 1, re] = cr[0:1]
            ci_ref[sb:sb + 1, re] = ci[0:1]
        y_ref[:, sb * SSM_SB_CH:(sb + 1) * SSM_SB_CH] = jnp.dot(
            bu_ref[...].astype(BF16), cblk_ref[sb], preferred_element_type=F32)
    y = jax.nn.gelu(y_ref[...] + d_ref[...] * u)
    z = jnp.dot(y.astype(BF16), wglu_ref[...], preferred_element_type=F32)
    o_ref[0] = (y * jax.nn.sigmoid(z)).astype(o_ref.dtype)

    @pl.when(i == pl.num_programs(1) - 1)
    def _():
        hr_out[0] = cr_ref[...]
        hi_out[0] = ci_ref[...]


def _s5(proj3, h0r, h0i, bblk, cblk, scan_consts, d_skip, wglu_bf16, ts):
    b, l, _ = proj3.shape
    st_spec = pl.BlockSpec((1, SSM_SB, SSM_SB_LANES), lambda s, i: (s, 0, 0))
    st_shape = jax.ShapeDtypeStruct((b, SSM_SB, SSM_SB_LANES), F32)
    return pl.pallas_call(
        functools.partial(_s5_kernel, ts=ts),
        out_shape=(jax.ShapeDtypeStruct((b, l, W_SSM), BF16), st_shape, st_shape),
        grid=(b, l // ts),
        in_specs=[
            pl.BlockSpec((1, ts, W_SSM), lambda s, i: (s, i, 1)),
            st_spec, st_spec,
            _const_spec(bblk.shape), _const_spec(cblk.shape), _const_spec(scan_consts.shape),
            _const_spec((1, W_SSM)), _const_spec((W_SSM, W_SSM)),
        ],
        out_specs=(pl.BlockSpec((1, ts, W_SSM), lambda s, i: (s, i, 0)), st_spec, st_spec),
        scratch_shapes=[
            pltpu.VMEM((ts, 2 * SSM_SB_LANES), F32),
            pltpu.VMEM((ts, 2 * SSM_SB_LANES), F32),
            pltpu.VMEM((ts, W_SSM), F32),
            pltpu.VMEM((SSM_SB, SSM_SB_LANES), F32),
            pltpu.VMEM((SSM_SB, SSM_SB_LANES), F32),
        ],
        compiler_params=_params("parallel", "arbitrary"),
        name="s5",
    )(proj3, h0r, h0i, bblk, cblk, scan_consts, d_skip, wglu_bf16)


def _s5_weights(a_re, a_im, log_step, b_re, b_im, c_re, c_im):
    dt = jnp.exp(log_step)[:, None]
    mag = jnp.exp(a_re * dt)
    abr = mag * jnp.cos(a_im * dt)
    abi = mag * jnp.sin(a_im * dt)
    nr, ni = abr - 1.0, abi
    den = a_re * a_re + a_im * a_im
    f_re = (nr * a_re + ni * a_im) / den
    f_im = (ni * a_re - nr * a_im) / den
    bbr = f_re[..., None] * b_re - f_im[..., None] * b_im
    bbi = f_re[..., None] * b_im + f_im[..., None] * b_re
    gl = N_SSM_GROUPS // SSM_SB
    eye = jnp.eye(gl, dtype=F32)

    def bdiag_in(x):
        x = x.reshape(SSM_SB, gl, SSM_STATE, SSM_GROUP)
        return jnp.einsum("sgpc,gh->sgchp", x, eye).reshape(SSM_SB, gl * SSM_GROUP, gl * SSM_STATE)

    def bdiag_out(x):
        x = x.reshape(SSM_SB, gl, SSM_GROUP, SSM_STATE)
        return jnp.einsum("sgcp,gh->sgphc", x, eye).reshape(SSM_SB, gl * SSM_STATE, gl * SSM_GROUP)

    bblk = jnp.concatenate([bdiag_in(bbr), bdiag_in(bbi)], axis=2).astype(BF16)
    cblk = jnp.concatenate([bdiag_out(c_re), -bdiag_out(c_im)], axis=1).astype(BF16)

    ar = abr.reshape(SSM_SB, SSM_SB_LANES)
    ai = abi.reshape(SSM_SB, SSM_SB_LANES)
    pw_r, pw_i = [ar], [ai]
    for _ in range(SCAN_ROWS - 1):
        pr, pi = pw_r[-1], pw_i[-1]
        pw_r.append(pr * ar - pi * ai)
        pw_i.append(pr * ai + pi * ar)
    row = jnp.arange(SCAN_ROWS)[None, :, None]

    def step(d):
        m = (row >= d).astype(F32)
        return pw_r[d - 1][:, None, :] * m, pw_i[d - 1][:, None, :] * m

    s1r, s1i = step(1)
    s2r, s2i = step(2)
    s4r, s4i = step(4)
    car_r = jnp.stack(pw_r, axis=1)
    car_i = jnp.stack(pw_i, axis=1)
    consts = jnp.stack([s1r, s1i, s2r, s2i, s4r, s4i, car_r, car_i], axis=1)
    return bblk, cblk, consts


def _merge_kernel(x_ref, oa_ref, ob_ref, ga_ref, gb_ref, pa_ref, pb_ref, wo_ref, g2_ref,
                  x2_ref, xnt_ref):
    ma = jnp.dot(oa_ref[0], pa_ref[...], preferred_element_type=F32)
    mb = jnp.dot(ob_ref[0], pb_ref[...], preferred_element_type=F32)
    merged = jax.nn.sigmoid(ga_ref[0]) * ma + jax.nn.sigmoid(gb_ref[0]) * mb
    x2 = x_ref[0] + jnp.dot(merged.astype(BF16), wo_ref[...], preferred_element_type=F32)
    x2_ref[0] = x2
    xnt_ref[...] = _rms(x2, g2_ref[...]).T.astype(BF16)


def _merge(x3, oa3, ob3, proj3, pa, pb, wo, g2, tm, pad_blocks):
    b, l, _ = x3.shape
    off = pad_blocks
    nblk = l // tm
    return pl.pallas_call(
        _merge_kernel,
        out_shape=(jax.ShapeDtypeStruct((b, l, D_MODEL), F32), jax.ShapeDtypeStruct((D_MODEL, b * l), BF16)),
        grid=(b, nblk),
        in_specs=[
            pl.BlockSpec((1, tm, D_MODEL), lambda s, i: (s, i, 0)),
            pl.BlockSpec((1, tm, W_POOL), lambda s, i: (s, i + off, 0)),
            pl.BlockSpec((1, tm, W_SSM), lambda s, i: (s, i + off, 0)),
            pl.BlockSpec((1, tm, D_MODEL), lambda s, i: (s, i + off, 1)),
            pl.BlockSpec((1, tm, D_MODEL), lambda s, i: (s, i + off, 2)),
            _const_spec((W_POOL, D_MODEL)), _const_spec((W_SSM, D_MODEL)),
            _const_spec((D_MODEL, D_MODEL)), _const_spec((1, D_MODEL)),
        ],
        out_specs=(pl.BlockSpec((1, tm, D_MODEL), lambda s, i: (s, i, 0)),
                   pl.BlockSpec((D_MODEL, tm), lambda s, i: (0, s * nblk + i))),
        compiler_params=_params("parallel", "parallel"),
        name="merge",
    )(x3, oa3, ob3, proj3, proj3, pa, pb, wo, g2)


def _query_kernel(xnt_ref, wqt_ref, k1_ref, k2_ref, s_ref):
    qt = jnp.dot(wqt_ref[...], xnt_ref[...], preferred_element_type=F32).astype(BF16)
    for h in range(PEER_HEADS):
        r0 = h * D_QUERY
        s_ref[h, 0] = jnp.dot(k1_ref[...], qt[r0:r0 + D_HALF], preferred_element_type=F32)
        s_ref[h, 1] = jnp.dot(k2_ref[...], qt[r0 + D_HALF:r0 + D_QUERY], preferred_element_type=F32)


def _query(xnt, wqt, k1, k2, tm):
    n = xnt.shape[1]
    return pl.pallas_call(
        _query_kernel,
        out_shape=jax.ShapeDtypeStruct((PEER_HEADS, 2, N_KEYS, n), F32),
        grid=(n // tm,),
        in_specs=[
            pl.BlockSpec((D_MODEL, tm), lambda i: (0, i)),
            _const_spec((PEER_HEADS * D_QUERY, D_MODEL)),
            _const_spec((N_KEYS, D_HALF)), _const_spec((N_KEYS, D_HALF)),
        ],
        out_specs=pl.BlockSpec((PEER_HEADS, 2, N_KEYS, tm), lambda i: (0, 0, 0, i)),
        compiler_params=_params("parallel"),
        name="query",
    )(xnt, wqt, k1, k2)


LANE = 128
SUBLANE = 8
PACK_ROWS = 16


def _oddeven_merge_sort_pairs(n):
    pairs = []
    p = 1
    while p < n:
        k = p
        while k >= 1:
            for j in range(k % p, n - k, 2 * k):
                for i in range(min(k, n - j - k)):
                    if (i + j) // (p * 2) == (i + j + k) // (p * 2):
                        pairs.append((i + j, i + j + k))
            k //= 2
        p *= 2
    return pairs


_SORT16_PAIRS = _oddeven_merge_sort_pairs(PEER_TOPK)


def _cmpx(v, i, j):
    a, b = v[i], v[j]
    v[i] = jnp.maximum(a, b)
    v[j] = jnp.minimum(a, b)


def _sort16(v):
    v = list(v)
    for i, j in _SORT16_PAIRS:
        _cmpx(v, i, j)
    return v


def _merge_top16(a, b):
    k = PEER_TOPK
    c = [a[i] if b[k - 1 - i] is None else jnp.maximum(a[i], b[k - 1 - i]) for i in range(k)]
    d = k // 2
    while d >= 1:
        for i in range(k):
            if not i & d:
                _cmpx(c, i, i + d)
        d //= 2
    return c


def _top16_values(rows):
    groups = [_sort16(rows[g:g + PEER_TOPK]) for g in range(0, len(rows), PEER_TOPK)]
    while len(groups) > 1:
        groups = [_merge_top16(groups[g], groups[g + 1]) for g in range(0, len(groups), 2)]
    return groups[0]


def _top16(val):
    rows = val.shape[0]
    rowid = lax.broadcasted_iota(jnp.int32, val.shape, 0).astype(F32)
    kshape = (PEER_TOPK,) + val.shape[1:]
    kid = lax.broadcasted_iota(jnp.int32, kshape, 0)

    def body(r, carry):
        val, rank, tops = carry
        m = jnp.max(val, axis=0, keepdims=True)
        idx = jnp.min(jnp.where(val == m, rowid, float(rows)), axis=0, keepdims=True)
        hit = rowid == idx
        rank = jnp.where(hit, lax.convert_element_type(r, F32), rank)
        val = jnp.where(hit, -jnp.inf, val)
        tops = jnp.where(kid == r, m, tops)
        return val, rank, tops

    init = (val, jnp.full(val.shape, float(PEER_TOPK), F32), jnp.zeros(kshape, F32))
    _, rank, tops = lax.fori_loop(0, PEER_TOPK, body, init)
    return rank, tops


def _outer16(a, b, op):
    return jnp.concatenate([op(a[r:r + 1], b) for r in range(PEER_TOPK)], axis=0)


def _gates_kernel(s_ref, n1_ref, e1_ref, r2_ref, e2_ref):
    k = PEER_TOPK
    s1 = [s_ref[0, 0, i] for i in range(N_KEYS)]
    s2 = [s_ref[0, 1, i] for i in range(N_KEYS)]
    v1 = _top16_values(s1)
    v2 = _top16_values(s2)
    width = [k // (r + 1) for r in range(k)]
    cell = [[v1[r1] + v2[r2] for r2 in range(width[r1])] for r1 in range(k)]
    top = cell[0]
    for r1 in range(1, k):
        top = _merge_top16(top, cell[r1] + [None] * (k - width[r1]))
    tau = top[k - 1]

    def count(vals, thr, cmp):
        n = jnp.zeros_like(thr)
        for v in vals:
            n = n + cmp(v, thr).astype(F32)
        return n

    x1 = [jnp.exp(v - v1[0]) for v in v1]
    x2 = [jnp.exp(v - v2[0]) for v in v2]
    z = jnp.zeros_like(tau)
    nrow = []
    for r1 in range(k):
        n = jnp.zeros_like(tau)
        for r2 in range(width[r1]):
            hit = cell[r1][r2] >= tau
            n = n + hit.astype(F32)
            z = z + jnp.where(hit, x1[r1] * x2[r2], 0.0)
        nrow.append(n)
    ncell = nrow[0]
    for n in nrow[1:]:
        ncell = ncell + n
    ge = lambda a, b: a >= b
    ambiguous = ((count(s1, v1[k - 1], ge) != float(k)) | (count(s2, v2[k - 1], ge) != float(k))
                 | (ncell != float(k)))
    for r1 in range(1, k):
        ambiguous = ambiguous | (v1[r1] + v2[width[r1]] >= tau)
    col = []
    for j in range(1, k + 1):
        c = jnp.full_like(tau, jnp.inf)
        for r1 in range(k // j):
            c = jnp.minimum(c, jnp.where(nrow[r1] >= float(j), v1[r1], jnp.inf))
        col.append(c)
    inv_z = 1.0 / z
    gt = lambda a, b: a > b
    for i in range(N_KEYS):
        n1_ref[0, i] = count(col, s1[i], lambda c, s: s >= c)
        r2_ref[0, i] = count(v2, s2[i], gt)
        e1_ref[0, i] = jnp.where(s1[i] >= v1[k - 1], jnp.exp(s1[i] - v1[0]) * inv_z, 0.0)
        e2_ref[0, i] = jnp.where(s2[i] >= v2[k - 1], jnp.exp(s2[i] - v2[0]), 0.0)

    @pl.when(jnp.max(ambiguous.astype(F32)) > 0.0)
    def _():
        t1 = s_ref[0, 0]
        t2 = s_ref[0, 1]
        rank1, w1 = _top16(t1)
        rank2, w2 = _top16(t2)
        crank, _ = _top16(_outer16(w1, w2, jnp.add))
        sel = (crank < float(k)).astype(F32)
        y1 = jnp.exp(w1 - w1[0:1])
        y2 = jnp.exp(w2 - w2[0:1])
        zz = jnp.sum(sel * _outer16(y1, y2, jnp.multiply), axis=0, keepdims=True)
        n1 = jnp.zeros_like(t1)
        for r in range(k):
            cnt = jnp.sum(sel[r * k:(r + 1) * k], axis=0, keepdims=True)
            n1 = jnp.where(rank1 == float(r), cnt, n1)
        n1_ref[0] = n1
        r2_ref[0] = rank2
        e1_ref[0] = jnp.exp(t1 - w1[0:1]) / zz
        e2_ref[0] = jnp.exp(t2 - w2[0:1])


def _gates(scores):
    n = scores.shape[-1]
    nt = n // LANE
    rows = SUBLANE
    s5d = scores.reshape(PEER_HEADS, 2, N_KEYS, nt, LANE)
    spec = pl.BlockSpec((1, N_KEYS, rows, LANE), lambda i, h: (h, 0, i, 0))
    shp = jax.ShapeDtypeStruct((PEER_HEADS, N_KEYS, nt, LANE), F32)
    outs = pl.pallas_call(
        _gates_kernel,
        out_shape=(shp, shp, shp, shp),
        grid=(nt // rows, PEER_HEADS),
        in_specs=[pl.BlockSpec((1, 2, N_KEYS, rows, LANE), lambda i, h: (h, 0, 0, i, 0))],
        out_specs=(spec, spec, spec, spec),
        compiler_params=_params("parallel", "parallel"),
        name="gates",
    )(s5d)
    return tuple(a.reshape(PEER_HEADS, N_KEYS, n) for a in outs)


def _peer_kernel(xnt_ref, x2_ref, u_ref, vt_ref, n1_ref, e1_ref, r2_ref, e2_ref, gf_ref,
                 o_ref, acc_ref, w_ref, coef_ref, *, ec, t):
    c = pl.program_id(1)

    @pl.when(c == 0)
    def _():
        coef_ref[...] = jnp.zeros((ec, t), BF16)

    kt = 256
    for lt in range(t // LANE):
        lanes = slice(lt * LANE, (lt + 1) * LANE)
        for j in range(ec // N_KEYS):
            n1b = [jnp.broadcast_to(n1_ref[h, 0, j:j + 1, lanes], (SUBLANE, LANE)) for h in range(PEER_HEADS)]
            e1b = [jnp.broadcast_to(e1_ref[h, 0, j:j + 1, lanes], (SUBLANE, LANE)) for h in range(PEER_HEADS)]
            for ib in range(N_KEYS // SUBLANE):
                rows = slice(ib * SUBLANE, (ib + 1) * SUBLANE)
                w = jnp.zeros((SUBLANE, LANE), F32)
                for h in range(PEER_HEADS):
                    hit = r2_ref[h, rows, lanes] < n1b[h]
                    w = w + jnp.where(hit, e1b[h] * e2_ref[h, rows, lanes], 0.0)
                w_ref[j * N_KEYS + ib * SUBLANE:j * N_KEYS + (ib + 1) * SUBLANE, lanes] = w
    prev = jnp.where(c == 0, 0.0, acc_ref[...])
    acc_ref[...] = prev + jnp.dot(vt_ref[:, 0:kt], coef_ref[0:kt, :], preferred_element_type=F32)
    for k0 in range(kt, ec, kt):
        acc_ref[...] += jnp.dot(vt_ref[:, k0:k0 + kt], coef_ref[k0:k0 + kt, :], preferred_element_type=F32)
    act = jax.nn.gelu(jnp.dot(u_ref[...], xnt_ref[...], preferred_element_type=F32))
    coef_ref[...] = (w_ref[...] * act).astype(BF16)

    @pl.when(c == pl.num_programs(1) - 1)
    def _():
        o_ref[...] = _rms(x2_ref[...] + acc_ref[...].T, gf_ref[...])


def _peer(xnt, x2, u_bf16, vt_bf16, n1, e1, r2, e2, gf, t, ec):
    n = xnt.shape[1]
    nc = N_EXPERTS // ec
    kc = ec // N_KEYS
    n1 = n1.reshape(PEER_HEADS, nc, kc, n)
    e1 = e1.reshape(PEER_HEADS, nc, kc, n)
    cur = lambda c: jnp.minimum(c, nc - 1)
    prv = lambda c: jnp.maximum(c - 1, 0)
    cspec = pl.BlockSpec((PEER_HEADS, 1, kc, t), lambda b, c: (0, cur(c), 0, b))
    gspec = pl.BlockSpec((PEER_HEADS, N_KEYS, t), lambda b, c: (0, 0, b))
    return pl.pallas_call(
        functools.partial(_peer_kernel, ec=ec, t=t),
        out_shape=jax.ShapeDtypeStruct((n, D_MODEL), F32),
        grid=(n // t, nc + 1),
        in_specs=[
            pl.BlockSpec((D_MODEL, t), lambda b, c: (0, b)),
            pl.BlockSpec((t, D_MODEL), lambda b, c: (b, 0)),
            pl.BlockSpec((ec, D_MODEL), lambda b, c: (cur(c), 0)),
            pl.BlockSpec((D_MODEL, ec), lambda b, c: (0, prv(c))),
            cspec, cspec, gspec, gspec,
            _const_spec((1, D_MODEL)),
        ],
        out_specs=pl.BlockSpec((t, D_MODEL), lambda b, c: (b, 0)),
        scratch_shapes=[
            pltpu.VMEM((D_MODEL, t), F32),
            pltpu.VMEM((ec, t), F32),
            pltpu.VMEM((ec, t), BF16),
        ],
        compiler_params=_params("parallel", "arbitrary"),
        name="peer",
    )(xnt, x2, u_bf16, vt_bf16, n1, e1, r2, e2, gf)


def _group(x_real, front, hist16, h0r, h0i, wts, *, pos0, t_in, t_seq, t_merge, t_tok):
    b, l, _ = x_real.shape
    if front is None:
        pad_rows = 0
        proj3 = _inproj(x_real.reshape(1, b * l, D_MODEL), None, wts["norm1_g"], wts["w_in"], t_in)
        proj3 = proj3.reshape(b, l, W_IN_COLS)
    else:
        pad_rows = front.shape[0]
        proj3 = _inproj(x_real, front, wts["norm1_g"], wts["w_in"], t_in)
    lp = pad_rows + l
    out_a = _pool(proj3, hist16, wts["pool_w"], wts["pool_scale"], t_seq, pos0)
    out_b, hr, hi = _s5(proj3, h0r, h0i, wts["bblk"], wts["cblk"], wts["scan"], wts["ssm_d"], wts["w_glu"], t_seq)
    n = b * l
    if pad_rows == 0:
        x_real, out_a, out_b, projm = (a.reshape(1, n, a.shape[-1]) for a in (x_real, out_a, out_b, proj3))
    else:
        projm = proj3
    x2, xnt = _merge(x_real, out_a, out_b, projm, wts["proj_a"], wts["proj_b"], wts["w_out"],
                     wts["norm2_g"], t_merge, pad_rows // t_merge)
    scores = _query(xnt, wts["peer_wqt"], wts["peer_key1"], wts["peer_key2"], t_tok)
    n1, e1, r2, e2 = _gates(scores)
    y = _peer(xnt, x2.reshape(n, D_MODEL), wts["peer_u"], wts["peer_vt"], n1, e1, r2, e2,
              wts["final_norm_g"], t_tok, 512)
    new_hist = proj3[:, lp - POOL_HIST:, :W_POOL]
    shp = (b, N_SSM_GROUPS, SSM_STATE)
    return y.reshape(b, l, D_MODEL), new_hist, hr.reshape(shp), hi.reshape(shp)


PROMPT_PAD = 512


def kernel(x_prompt, x_sample, cache_pool, state_ssm_re, state_ssm_im, meta_tokens, norm1_g, w_in, pool_w, pool_scale, ssm_a_re, ssm_a_im, ssm_log_step, ssm_b_re, ssm_b_im, ssm_c_re, ssm_c_im, ssm_d, w_glu, proj_a, proj_b, w_out, norm2_g, peer_wq, peer_key1, peer_key2, peer_u, peer_v, final_norm_g):
    depth = norm1_g.shape[0]
    assert depth == 1
    bp, seq, _ = x_prompt.shape
    bs, dseq, _ = x_sample.shape
    past_len = 1024

    bblk, cblk, scan = _s5_weights(ssm_a_re[0], ssm_a_im[0], ssm_log_step[0], ssm_b_re[0], ssm_b_im[0],
                                   ssm_c_re[0], ssm_c_im[0])
    wts = {
        "norm1_g": norm1_g[0][None], "w_in": w_in[0].astype(BF16),
        "pool_w": pool_w[0].astype(BF16), "pool_scale": pool_scale[0][None],
        "bblk": bblk, "cblk": cblk, "scan": scan, "ssm_d": ssm_d[0][None], "w_glu": w_glu[0].astype(BF16),
        "proj_a": proj_a[0].astype(BF16), "proj_b": proj_b[0].astype(BF16), "w_out": w_out[0].astype(BF16),
        "norm2_g": norm2_g[0][None], "peer_wqt": peer_wq[0].astype(BF16).T,
        "peer_key1": peer_key1[0].astype(BF16), "peer_key2": peer_key2[0].astype(BF16),
        "peer_u": peer_u[0].astype(BF16), "peer_vt": peer_v[0].astype(BF16).T,
        "final_norm_g": final_norm_g[None],
    }

    front = jnp.concatenate([jnp.zeros((PROMPT_PAD - N_META, D_MODEL), x_prompt.dtype), meta_tokens], axis=0)
    zst = jnp.zeros((bp, SSM_SB, SSM_SB_LANES), F32)
    yp, pool_p, re_p, im_p = _group(
        x_prompt, front, jnp.zeros((bp, POOL_HIST + 1, W_POOL), F32), zst, zst, wts,
        pos0=1 - (PROMPT_PAD - N_META), t_in=PROMPT_PAD, t_seq=256, t_merge=256, t_tok=512)

    hist16 = jnp.pad(cache_pool[0], ((0, 0), (1, 0), (0, 0)))
    ys, pool_s, re_s, im_s = _group(
        x_sample, None, hist16,
        state_ssm_re[0].reshape(bs, SSM_SB, SSM_SB_LANES), state_ssm_im[0].reshape(bs, SSM_SB, SSM_SB_LANES),
        wts, pos0=1 + past_len, t_in=512, t_seq=dseq, t_merge=256, t_tok=512)

    return (yp, ys, pool_p[None], re_p[None], im_p[None], pool_s[None], re_s[None], im_s[None])
```

```python
import functools

import jax
import jax.numpy as jnp
from jax import lax
from jax.experimental import pallas as pl
from jax.experimental.pallas import tpu as pltpu

D_MODEL = 2048
N_META = 16
W_POOL = 1024
W_SSM = 1024
POOL_SIZES = (2, 4, 8, 16)
POOL_GROUP = 256
POOL_HIST = 15
N_SSM_GROUPS = 64
SSM_GROUP = 16
SSM_STATE = 64
PEER_HEADS = 8
PEER_TOPK = 16
N_KEYS = 128
N_EXPERTS = N_KEYS * N_KEYS
D_QUERY = 256
D_HALF = 128
W_IN_COLS = W_POOL + W_SSM + 2 * D_MODEL
NORM_EPS = 1e-6

F32 = jnp.float32
BF16 = jnp.bfloat16

SSM_SB = 4
SSM_SB_CH = W_SSM // SSM_SB
SSM_SB_LANES = (N_SSM_GROUPS // SSM_SB) * SSM_STATE
SCAN_ROWS = 8
SCAN_LANES = 256
SCAN_UNROLL = 8
CARRY_LANES = 512

VMEM_LIMIT = 56 * 1024 * 1024


def _const_spec(shape):
    nd = len(shape)
    return pl.BlockSpec(shape, lambda *_: (0,) * nd, pipeline_mode=pl.Buffered(1))


def _params(*sem, flags=None):
    return pltpu.CompilerParams(dimension_semantics=sem, vmem_limit_bytes=VMEM_LIMIT, flags=flags)


def _rms(x, g):
    return x * lax.rsqrt(jnp.mean(x * x, axis=-1, keepdims=True) + NORM_EPS) * g


def _inproj_kernel(x_ref, f_ref, g_ref, w_ref, o_ref, xn_ref, *, front_blocks):
    i = pl.program_id(1)
    first_col = pl.program_id(2) == 0

    @pl.when(first_col & (i >= front_blocks))
    def _():
        xn_ref[...] = _rms(x_ref[0], g_ref[...]).astype(BF16)

    if front_blocks:
        @pl.when(first_col & (i < front_blocks))
        def _():
            xn_ref[...] = _rms(f_ref[...], g_ref[...]).astype(BF16)

    o_ref[0] = jnp.dot(xn_ref[...], w_ref[...], preferred_element_type=F32)


def _inproj(x3, front, g, w_bf16, tm):
    b, l, _ = x3.shape
    cb = 2048
    fb = 0 if front is None else 1
    if front is None:
        front = jnp.zeros((SUBLANE, D_MODEL), F32)
    return pl.pallas_call(
        functools.partial(_inproj_kernel, front_blocks=fb),
        out_shape=jax.ShapeDtypeStruct((b, fb * tm + l, W_IN_COLS), F32),
        grid=(b, fb + l // tm, W_IN_COLS // cb),
        in_specs=[
            pl.BlockSpec((1, tm, D_MODEL), lambda s, i, j: (s, jnp.maximum(i - fb, 0), 0)),
            _const_spec(front.shape),
            _const_spec((1, D_MODEL)),
            pl.BlockSpec((D_MODEL, cb), lambda s, i, j: (0, j)),
        ],
        out_specs=pl.BlockSpec((1, tm, cb), lambda s, i, j: (s, i, j)),
        scratch_shapes=[pltpu.VMEM((tm, D_MODEL), BF16)],
        compiler_params=_params("parallel", "arbitrary", "arbitrary"),
        name="inproj",
    )(x3, front, g, w_bf16)


def _pool_kernel(u_ref, hist_ref, pw_ref, ps_ref, o_ref, ext_ref, *, tp, pos0):
    i = pl.program_id(1)
    hrows = POOL_HIST + 1

    @pl.when(i == 0)
    def _():
        ext_ref[0:hrows, :] = hist_ref[0]

    @pl.when(i > 0)
    def _():
        ext_ref[0:hrows, :] = ext_ref[tp:tp + hrows, :]

    ext_ref[hrows:hrows + tp, :] = u_ref[0]
    row = lax.broadcasted_iota(jnp.int32, (tp, 1), 0)
    avail = i * tp + row + pos0
    for gi, w in enumerate(POOL_SIZES):
        c0 = gi * POOL_GROUP
        u = ext_ref[hrows:hrows + tp, c0:c0 + POOL_GROUP]
        acc = u
        for j in range(1, w):
            acc = acc + ext_ref[hrows - j:hrows - j + tp, c0:c0 + POOL_GROUP]
        cnt = jnp.maximum(jnp.minimum(avail, w), 1).astype(F32)
        pooled = acc / cnt - u
        mixed = jnp.dot(pooled.astype(BF16), pw_ref[gi], preferred_element_type=F32)
        o_ref[0, :, c0:c0 + POOL_GROUP] = (mixed * ps_ref[:, c0:c0 + POOL_GROUP]).astype(o_ref.dtype)


def _pool(proj3, hist16, pw_bf16, pscale, tp, pos0):
    b, l, _ = proj3.shape
    return pl.pallas_call(
        functools.partial(_pool_kernel, tp=tp, pos0=pos0),
        out_shape=jax.ShapeDtypeStruct((b, l, W_POOL), BF16),
        grid=(b, l // tp),
        in_specs=[
            pl.BlockSpec((1, tp, W_POOL), lambda s, i: (s, i, 0)),
            pl.BlockSpec((1, POOL_HIST + 1, W_POOL), lambda s, i: (s, 0, 0)),
            _const_spec((len(POOL_SIZES), POOL_GROUP, POOL_GROUP)),
            _const_spec((1, W_POOL)),
        ],
        out_specs=pl.BlockSpec((1, tp, W_POOL), lambda s, i: (s, i, 0)),
        scratch_shapes=[pltpu.VMEM((POOL_HIST + 1 + tp, W_POOL), F32)],
        compiler_params=_params("parallel", "arbitrary"),
        name="pool",
    )(proj3, hist16, pw_bf16, pscale)


def _s5_kernel(u_ref, h0r_ref, h0i_ref, bblk_ref, cblk_ref, sc_ref, d_ref, wglu_ref,
               o_ref, hr_out, hi_out, bu_ref, xs_ref, y_ref, cr_ref, ci_ref, *, ts):
    i = pl.program_id(1)

    @pl.when(i == 0)
    def _():
        cr_ref[...] = h0r_ref[0]
        ci_ref[...] = h0i_ref[0]

    u = u_ref[0]
    ub = u.astype(BF16)
    for sb in range(SSM_SB):
        bu_ref[...] = jnp.dot(ub[:, sb * SSM_SB_CH:(sb + 1) * SSM_SB_CH], bblk_ref[sb],
                              preferred_element_type=F32)
        for lc in range(SSM_SB_LANES // SCAN_LANES):
            re = slice(lc * SCAN_LANES, (lc + 1) * SCAN_LANES)
            im = slice(SSM_SB_LANES + lc * SCAN_LANES, SSM_SB_LANES + (lc + 1) * SCAN_LANES)
            a1r, a1i, a2r, a2i, a4r, a4i = [sc_ref[sb, k, :, re] for k in range(6)]
            steps = ((a1r, a1i, 1), (a2r, a2i, 2), (a4r, a4i, 4))

            def local_scan(r, _, re=re, im=im, steps=steps):
                rows = pl.ds(pl.multiple_of(r * SCAN_ROWS, SCAN_ROWS), SCAN_ROWS)
                xr = bu_ref[rows, re]
                xi = bu_ref[rows, im]
                for ar, ai, d in steps:
                    sr = pltpu.roll(xr, d, 0)
                    si = pltpu.roll(xi, d, 0)
                    xr, xi = xr + ar * sr - ai * si, xi + ar * si + ai * sr
                xs_ref[rows, re] = xr
                xs_ref[rows, im] = xi
                return 0

            lax.fori_loop(0, ts // SCAN_ROWS, local_scan, 0, unroll=SCAN_UNROLL)
        for lc in range(SSM_SB_LANES // CARRY_LANES):
            re = slice(lc * CARRY_LANES, (lc + 1) * CARRY_LANES)
            im = slice(SSM_SB_LANES + lc * CARRY_LANES, SSM_SB_LANES + (lc + 1) * CARRY_LANES)
            pr = sc_ref[sb, 6, :, re]
            pi = sc_ref[sb, 7, :, re]
            cr0 = jnp.broadcast_to(cr_ref[sb:sb + 1, re], (SCAN_ROWS, CARRY_LANES))
            ci0 = jnp.broadcast_to(ci_ref[sb:sb + 1, re], (SCAN_ROWS, CARRY_LANES))

            def add_carry(r, carry, re=re, im=im, pr=pr, pi=pi):
                cr, ci = carry
                rows = pl.ds(pl.multiple_of(r * SCAN_ROWS, SCAN_ROWS), SCAN_ROWS)
                hr = xs_ref[rows, re] + pr * cr - pi * ci
                hi = xs_ref[rows, im] + pr * ci + pi * cr
                bu_ref[rows, re] = hr
                bu_ref[rows, im] = hi
                last = slice(SCAN_ROWS - 1, SCAN_ROWS)
                return (jnp.broadcast_to(hr[last], (SCAN_ROWS, CARRY_LANES)),
                        jnp.broadcast_to(hi[last], (SCAN_ROWS, CARRY_LANES)))

            cr, ci = lax.fori_loop(0, ts // SCAN_ROWS, add_carry, (cr0, ci0), unroll=2)
            cr_ref[sb:sb + 1, re] = cr[0:1]
            ci_ref[sb:sb + 1, re] = ci[0:1]
        y_ref[:, sb * SSM_SB_CH:(sb + 1) * SSM_SB_CH] = jnp.dot(
            bu_ref[...].astype(BF16), cblk_ref[sb], preferred_element_type=F32)
    y = jax.nn.gelu(y_ref[...] + d_ref[...] * u)
    z = jnp.dot(y.astype(BF16), wglu_ref[...], preferred_element_type=F32)
    o_ref[0] = (y * jax.nn.sigmoid(z)).astype(o_ref.dtype)

    @pl.when(i == pl.num_programs(1) - 1)
    def _():
        hr_out[0] = cr_ref[...]
        hi_out[0] = ci_ref[...]


def _s5(proj3, h0r, h0i, bblk, cblk, scan_consts, d_skip, wglu_bf16, ts):
    b, l, _ = proj3.shape
    st_spec = pl.BlockSpec((1, SSM_SB, SSM_SB_LANES), lambda s, i: (s, 0, 0))
    st_shape = jax.ShapeDtypeStruct((b, SSM_SB, SSM_SB_LANES), F32)
    return pl.pallas_call(
        functools.partial(_s5_kernel, ts=ts),
        out_shape=(jax.ShapeDtypeStruct((b, l, W_SSM), BF16), st_shape, st_shape),
        grid=(b, l // ts),
        in_specs=[
            pl.BlockSpec((1, ts, W_SSM), lambda s, i: (s, i, 1)),
            st_spec, st_spec,
            _const_spec(bblk.shape), _const_spec(cblk.shape), _const_spec(scan_consts.shape),
            _const_spec((1, W_SSM)), _const_spec((W_SSM, W_SSM)),
        ],
        out_specs=(pl.BlockSpec((1, ts, W_SSM), lambda s, i: (s, i, 0)), st_spec, st_spec),
        scratch_shapes=[
            pltpu.VMEM((ts, 2 * SSM_SB_LANES), F32),
            pltpu.VMEM((ts, 2 * SSM_SB_LANES), F32),
            pltpu.VMEM((ts, W_SSM), F32),
            pltpu.VMEM((SSM_SB, SSM_SB_LANES), F32),
            pltpu.VMEM((SSM_SB, SSM_SB_LANES), F32),
        ],
        compiler_params=_params("parallel", "arbitrary"),
        name="s5",
    )(proj3, h0r, h0i, bblk, cblk, scan_consts, d_skip, wglu_bf16)


def _s5_weights(a_re, a_im, log_step, b_re, b_im, c_re, c_im):
    dt = jnp.exp(log_step)[:, None]
    mag = jnp.exp(a_re * dt)
    abr = mag * jnp.cos(a_im * dt)
    abi = mag * jnp.sin(a_im * dt)
    nr, ni = abr - 1.0, abi
    den = a_re * a_re + a_im * a_im
    f_re = (nr * a_re + ni * a_im) / den
    f_im = (ni * a_re - nr * a_im) / den
    bbr = f_re[..., None] * b_re - f_im[..., None] * b_im
    bbi = f_re[..., None] * b_im + f_im[..., None] * b_re
    gl = N_SSM_GROUPS // SSM_SB
    eye = jnp.eye(gl, dtype=F32)

    def bdiag_in(x):
        x = x.reshape(SSM_SB, gl, SSM_STATE, SSM_GROUP)
        return jnp.einsum("sgpc,gh->sgchp", x, eye).reshape(SSM_SB, gl * SSM_GROUP, gl * SSM_STATE)

    def bdiag_out(x):
        x = x.reshape(SSM_SB, gl, SSM_GROUP, SSM_STATE)
        return jnp.einsum("sgcp,gh->sgphc", x, eye).reshape(SSM_SB, gl * SSM_STATE, gl * SSM_GROUP)

    bblk = jnp.concatenate([bdiag_in(bbr), bdiag_in(bbi)], axis=2).astype(BF16)
    cblk = jnp.concatenate([bdiag_out(c_re), -bdiag_out(c_im)], axis=1).astype(BF16)

    ar = abr.reshape(SSM_SB, SSM_SB_LANES)
    ai = abi.reshape(SSM_SB, SSM_SB_LANES)
    pw_r, pw_i = [ar], [ai]
    for _ in range(SCAN_ROWS - 1):
        pr, pi = pw_r[-1], pw_i[-1]
        pw_r.append(pr * ar - pi * ai)
        pw_i.append(pr * ai + pi * ar)
    row = jnp.arange(SCAN_ROWS)[None, :, None]

    def step(d):
        m = (row >= d).astype(F32)
        return pw_r[d - 1][:, None, :] * m, pw_i[d - 1][:, None, :] * m

    s1r, s1i = step(1)
    s2r, s2i = step(2)
    s4r, s4i = step(4)
    car_r = jnp.stack(pw_r, axis=1)
    car_i = jnp.stack(pw_i, axis=1)
    consts = jnp.stack([s1r, s1i, s2r, s2i, s4r, s4i, car_r, car_i], axis=1)
    return bblk, cblk, consts


def _merge_kernel(x_ref, oa_ref, ob_ref, ga_ref, gb_ref, pa_ref, pb_ref, wo_ref, g2_ref,
                  x2_ref, xnt_ref):
    ma = jnp.dot(oa_ref[0], pa_ref[...], preferred_element_type=F32)
    mb = jnp.dot(ob_ref[0], pb_ref[...], preferred_element_type=F32)
    merged = jax.nn.sigmoid(ga_ref[0]) * ma + jax.nn.sigmoid(gb_ref[0]) * mb
    x2 = x_ref[0] + jnp.dot(merged.astype(BF16), wo_ref[...], preferred_element_type=F32)
    x2_ref[0] = x2
    xnt_ref[...] = _rms(x2, g2_ref[...]).T.astype(BF16)


def _merge(x3, oa3, ob3, proj3, pa, pb, wo, g2, tm, pad_blocks):
    b, l, _ = x3.shape
    off = pad_blocks
    nblk = l // tm
    return pl.pallas_call(
        _merge_kernel,
        out_shape=(jax.ShapeDtypeStruct((b, l, D_MODEL), F32), jax.ShapeDtypeStruct((D_MODEL, b * l), BF16)),
        grid=(b, nblk),
        in_specs=[
            pl.BlockSpec((1, tm, D_MODEL), lambda s, i: (s, i, 0)),
            pl.BlockSpec((1, tm, W_POOL), lambda s, i: (s, i + off, 0)),
            pl.BlockSpec((1, tm, W_SSM), lambda s, i: (s, i + off, 0)),
            pl.BlockSpec((1, tm, D_MODEL), lambda s, i: (s, i + off, 1)),
            pl.BlockSpec((1, tm, D_MODEL), lambda s, i: (s, i + off, 2)),
            _const_spec((W_POOL, D_MODEL)), _const_spec((W_SSM, D_MODEL)),
            _const_spec((D_MODEL, D_MODEL)), _const_spec((1, D_MODEL)),
        ],
        out_specs=(pl.BlockSpec((1, tm, D_MODEL), lambda s, i: (s, i, 0)),
                   pl.BlockSpec((D_MODEL, tm), lambda s, i: (0, s * nblk + i))),
        compiler_params=_params("parallel", "parallel"),
        name="merge",
    )(x3, oa3, ob3, proj3, proj3, pa, pb, wo, g2)


LANE = 128
SUBLANE = 8
PACK_ROWS = 16
PEER_TOKENS = 512
KEY_ROWS = PEER_TOKENS // LANE
SLAB_ROWS = N_KEYS * KEY_ROWS
HEAD_PAIR = SUBLANE // KEY_ROWS
SLAB_PAD = SUBLANE


def _slab_shift(kind, head):
    return kind + 2 * (head % HEAD_PAIR)


def _query_kernel(xnt_ref, wqt_ref, k1_ref, k2_ref, s_ref):
    qt = jnp.dot(wqt_ref[...], xnt_ref[...], preferred_element_type=F32).astype(BF16)
    for h in range(PEER_HEADS):
        r0 = h * D_QUERY
        halves = (jnp.dot(k1_ref[...], qt[r0:r0 + D_HALF], preferred_element_type=F32),
                  jnp.dot(k2_ref[...], qt[r0 + D_HALF:r0 + D_QUERY], preferred_element_type=F32))
        for half, s in enumerate(halves):
            for j in range(KEY_ROWS):
                s_ref[h, half, 0, pl.ds(j, N_KEYS, stride=KEY_ROWS), :] = s[:, j * 128:(j + 1) * 128]


def _query(xnt, wqt, k1, k2):
    n = xnt.shape[1]
    nblk = n // PEER_TOKENS
    return pl.pallas_call(
        _query_kernel,
        out_shape=jax.ShapeDtypeStruct((PEER_HEADS, 2, nblk, SLAB_ROWS, 128), F32),
        grid=(nblk,),
        in_specs=[
            pl.BlockSpec((D_MODEL, PEER_TOKENS), lambda i: (0, i)),
            _const_spec((PEER_HEADS * D_QUERY, D_MODEL)),
            _const_spec((N_KEYS, D_HALF)), _const_spec((N_KEYS, D_HALF)),
        ],
        out_specs=pl.BlockSpec((PEER_HEADS, 2, 1, SLAB_ROWS, 128), lambda i: (0, 0, i, 0, 0)),
        compiler_params=_params("parallel"),
        name="query",
    )(xnt, wqt, k1, k2)


def _oddeven_merge_sort_pairs(n):
    pairs = []
    p = 1
    while p < n:
        k = p
        while k >= 1:
            for j in range(k % p, n - k, 2 * k):
                for i in range(min(k, n - j - k)):
                    if (i + j) // (p * 2) == (i + j + k) // (p * 2):
                        pairs.append((i + j, i + j + k))
            k //= 2
        p *= 2
    return pairs


_SORT16_PAIRS = _oddeven_merge_sort_pairs(PEER_TOPK)


def _cmpx(v, i, j):
    a, b = v[i], v[j]
    v[i] = jnp.maximum(a, b)
    v[j] = jnp.minimum(a, b)


def _sort16(v):
    v = list(v)
    for i, j in _SORT16_PAIRS:
        _cmpx(v, i, j)
    return v


def _merge_top16(a, b):
    k = PEER_TOPK
    c = [a[i] if b[k - 1 - i] is None else jnp.maximum(a[i], b[k - 1 - i]) for i in range(k)]
    d = k // 2
    while d >= 1:
        for i in range(k):
            if not i & d:
                _cmpx(c, i, i + d)
        d //= 2
    return c


def _top16_values(rows):
    groups = [_sort16(rows[g:g + PEER_TOPK]) for g in range(0, len(rows), PEER_TOPK)]
    while len(groups) > 1:
        groups = [_merge_top16(groups[g], groups[g + 1]) for g in range(0, len(groups), 2)]
    return groups[0]


def _top16(val):
    rows = val.shape[0]
    rowid = lax.broadcasted_iota(jnp.int32, val.shape, 0).astype(F32)
    kshape = (PEER_TOPK,) + val.shape[1:]
    kid = lax.broadcasted_iota(jnp.int32, kshape, 0)

    def body(r, carry):
        val, rank, tops = carry
        m = jnp.max(val, axis=0, keepdims=True)
        idx = jnp.min(jnp.where(val == m, rowid, float(rows)), axis=0, keepdims=True)
        hit = rowid == idx
        rank = jnp.where(hit, lax.convert_element_type(r, F32), rank)
        val = jnp.where(hit, -jnp.inf, val)
        tops = jnp.where(kid == r, m, tops)
        return val, rank, tops

    init = (val, jnp.full(val.shape, float(PEER_TOPK), F32), jnp.zeros(kshape, F32))
    _, rank, tops = lax.fori_loop(0, PEER_TOPK, body, init)
    return rank, tops


def _outer16(a, b, op):
    return jnp.concatenate([op(a[r:r + 1], b) for r in range(PEER_TOPK)], axis=0)


def _gates_kernel(s_ref, n1_ref, e1_ref, r2_ref, e2_ref):
    k = PEER_TOPK

    def load_key(half, i):
        rows = slice(i * KEY_ROWS, (i + 1) * KEY_ROWS)
        return jnp.concatenate([s_ref[hh, half, 0, rows, :] for hh in range(HEAD_PAIR)], axis=0)

    def store_key(ref, i, val, kind=None):
        for hh in range(HEAD_PAIR):
            shift = 0 if kind is None else _slab_shift(kind, hh)
            rows = slice(i * KEY_ROWS + shift, (i + 1) * KEY_ROWS + shift)
            ref[hh, 0, rows, :] = val[hh * KEY_ROWS:(hh + 1) * KEY_ROWS]

    for ref in (r2_ref, e2_ref):
        for hh in range(HEAD_PAIR):
            ref[hh, 0, 0:SUBLANE, :] = jnp.zeros((SUBLANE, LANE), F32)
            ref[hh, 0, SLAB_ROWS:SLAB_ROWS + SLAB_PAD, :] = jnp.zeros((SLAB_PAD, LANE), F32)
    s1 = [load_key(0, i) for i in range(N_KEYS)]
    s2 = [load_key(1, i) for i in range(N_KEYS)]
    v1 = _top16_values(s1)
    v2 = _top16_values(s2)
    width = [k // (r + 1) for r in range(k)]
    cell = [[v1[r1] + v2[r2] for r2 in range(width[r1])] for r1 in range(k)]
    top = cell[0]
    for r1 in range(1, k):
        top = _merge_top16(top, cell[r1] + [None] * (k - width[r1]))
    tau = top[k - 1]

    def count(vals, thr, cmp):
        n = jnp.zeros_like(thr)
        for v in vals:
            n = n + cmp(v, thr).astype(F32)
        return n

    x1 = [jnp.exp(v - v1[0]) for v in v1]
    x2 = [jnp.exp(v - v2[0]) for v in v2]
    z = jnp.zeros_like(tau)
    nrow = []
    for r1 in range(k):
        n = jnp.zeros_like(tau)
        for r2 in range(width[r1]):
            hit = cell[r1][r2] >= tau
            n = n + hit.astype(F32)
            z = z + jnp.where(hit, x1[r1] * x2[r2], 0.0)
        nrow.append(n)
    ncell = nrow[0]
    for n in nrow[1:]:
        ncell = ncell + n
    ge = lambda a, b: a >= b
    ambiguous = ((count(s1, v1[k - 1], ge) != float(k)) | (count(s2, v2[k - 1], ge) != float(k))
                 | (ncell != float(k)))
    for r1 in range(1, k):
        ambiguous = ambiguous | (v1[r1] + v2[width[r1]] >= tau)
    col = []
    for j in range(1, k + 1):
        c = jnp.full_like(tau, jnp.inf)
        for r1 in range(k // j):
            c = jnp.minimum(c, jnp.where(nrow[r1] >= float(j), v1[r1], jnp.inf))
        col.append(c)
    inv_z = 1.0 / z
    gt = lambda a, b: a > b
    for i in range(N_KEYS):
        store_key(n1_ref, i, count(col, s1[i], lambda c, s: s >= c))
        store_key(r2_ref, i, count(v2, s2[i], gt), 0)
        store_key(e1_ref, i, jnp.where(s1[i] >= v1[k - 1], jnp.exp(s1[i] - v1[0]) * inv_z, 0.0))
        store_key(e2_ref, i, jnp.where(s2[i] >= v2[k - 1], jnp.exp(s2[i] - v2[0]), 0.0), 1)

    @pl.when(jnp.max(ambiguous.astype(F32)) > 0.0)
    def _():
        t1 = jnp.stack(s1, axis=0)
        t2 = jnp.stack(s2, axis=0)
        rank1, w1 = _top16(t1)
        rank2, w2 = _top16(t2)
        crank, _ = _top16(_outer16(w1, w2, jnp.add))
        sel = (crank < float(k)).astype(F32)
        y1 = jnp.exp(w1 - w1[0:1])
        y2 = jnp.exp(w2 - w2[0:1])
        zz = jnp.sum(sel * _outer16(y1, y2, jnp.multiply), axis=0, keepdims=True)
        n1 = jnp.zeros_like(t1)
        for r in range(k):
            cnt = jnp.sum(sel[r * k:(r + 1) * k], axis=0, keepdims=True)
            n1 = jnp.where(rank1 == float(r), cnt, n1)
        g1 = jnp.exp(t1 - w1[0:1]) / zz
        g2 = jnp.exp(t2 - w2[0:1])
        for i in range(N_KEYS):
            store_key(n1_ref, i, n1[i])
            store_key(r2_ref, i, rank2[i], 0)
            store_key(e1_ref, i, g1[i])
            store_key(e2_ref, i, g2[i], 1)


def _gates(scores):
    nblk = scores.shape[2]
    spec = pl.BlockSpec((HEAD_PAIR, 1, SLAB_ROWS, LANE), lambda i, g: (g, i, 0, 0))
    shp = jax.ShapeDtypeStruct((PEER_HEADS, nblk, SLAB_ROWS, LANE), F32)
    pspec = pl.BlockSpec((HEAD_PAIR, 1, SLAB_ROWS + SLAB_PAD, LANE), lambda i, g: (g, i, 0, 0))
    pshp = jax.ShapeDtypeStruct((PEER_HEADS, nblk, SLAB_ROWS + SLAB_PAD, LANE), F32)
    return pl.pallas_call(
        _gates_kernel,
        out_shape=(shp, shp, pshp, pshp),
        grid=(nblk, PEER_HEADS // HEAD_PAIR),
        in_specs=[pl.BlockSpec((HEAD_PAIR, 2, 1, SLAB_ROWS, LANE), lambda i, g: (g, 0, i, 0, 0))],
        out_specs=(spec, spec, pspec, pspec),
        compiler_params=_params("parallel", "parallel"),
        name="gates",
    )(scores)


def _peer_kernel(xnt_ref, x2_ref, u_ref, vt_ref, n1_ref, e1_ref, r2_ref, e2_ref, gf_ref,
                 o_ref, acc_ref, w_ref, coef_ref, *, ec, t):
    c = pl.program_id(1)

    @pl.when(c == 0)
    def _():
        coef_ref[...] = jnp.zeros((ec, t), BF16)

    kt = 256
    for lt in range(t // LANE):
        lanes = slice(lt * LANE, (lt + 1) * LANE)
        for j in range(ec // N_KEYS):
            krow = slice(j * KEY_ROWS + lt, j * KEY_ROWS + lt + 1)
            n1b = [jnp.broadcast_to(n1_ref[h, 0, krow, :], (SUBLANE, LANE)) for h in range(PEER_HEADS)]
            e1b = [jnp.broadcast_to(e1_ref[h, 0, krow, :], (SUBLANE, LANE)) for h in range(PEER_HEADS)]
            for ib in range(N_KEYS // SUBLANE):
                row0 = ib * SUBLANE * KEY_ROWS + lt
                rows = lambda kind, h: pl.ds(row0 + _slab_shift(kind, h), SUBLANE, stride=KEY_ROWS)
                w = jnp.zeros((SUBLANE, LANE), F32)
                for h in range(PEER_HEADS):
                    hit = r2_ref[h, 0, rows(0, h), :] < n1b[h]
                    w = w + jnp.where(hit, e1b[h] * e2_ref[h, 0, rows(1, h), :], 0.0)
                w_ref[j * N_KEYS + ib * SUBLANE:j * N_KEYS + (ib + 1) * SUBLANE, lanes] = w
    prev = jnp.where(c == 0, 0.0, acc_ref[...])
    acc_ref[...] = prev + jnp.dot(vt_ref[:, 0:kt], coef_ref[0:kt, :], preferred_element_type=F32)
    for k0 in range(kt, ec, kt):
        acc_ref[...] += jnp.dot(vt_ref[:, k0:k0 + kt], coef_ref[k0:k0 + kt, :], preferred_element_type=F32)
    act = jax.nn.gelu(jnp.dot(u_ref[...], xnt_ref[...], preferred_element_type=F32))
    coef_ref[...] = (w_ref[...] * act).astype(BF16)

    @pl.when(c == pl.num_programs(1) - 1)
    def _():
        o_ref[...] = _rms(x2_ref[...] + acc_ref[...].T, gf_ref[...])


def _peer(xnt, x2, u_bf16, vt_bf16, n1, e1, r2, e2, gf, t, ec):
    n = xnt.shape[1]
    nc = N_EXPERTS // ec
    assert t == PEER_TOKENS
    kc = ec // N_KEYS
    cur = lambda c: jnp.minimum(c, nc - 1)
    prv = lambda c: jnp.maximum(c - 1, 0)
    cspec = pl.BlockSpec((PEER_HEADS, 1, kc * KEY_ROWS, LANE), lambda b, c: (0, b, cur(c), 0))
    gspec = pl.BlockSpec((PEER_HEADS, 1, SLAB_ROWS + SLAB_PAD, LANE), lambda b, c: (0, b, 0, 0))
    return pl.pallas_call(
        functools.partial(_peer_kernel, ec=ec, t=t),
        out_shape=jax.ShapeDtypeStruct((n, D_MODEL), F32),
        grid=(n // t, nc + 1),
        in_specs=[
            pl.BlockSpec((D_MODEL, t), lambda b, c: (0, b)),
            pl.BlockSpec((t, D_MODEL), lambda b, c: (b, 0)),
            pl.BlockSpec((ec, D_MODEL), lambda b, c: (cur(c), 0)),
            pl.BlockSpec((D_MODEL, ec), lambda b, c: (0, prv(c))),
            cspec, cspec, gspec, gspec,
            _const_spec((1, D_MODEL)),
        ],
        out_specs=pl.BlockSpec((t, D_MODEL), lambda b, c: (b, 0)),
        scratch_shapes=[
            pltpu.VMEM((D_MODEL, t), F32),
            pltpu.VMEM((ec, t), F32),
            pltpu.VMEM((ec, t), BF16),
        ],
        compiler_params=_params("parallel", "arbitrary"),
        name="peer",
    )(xnt, x2, u_bf16, vt_bf16, n1, e1, r2, e2, gf)


def _group(x_real, front, hist16, h0r, h0i, wts, *, pos0, t_in, t_seq, t_merge, t_tok):
    b, l, _ = x_real.shape
    if front is None:
        pad_rows = 0
        proj3 = _inproj(x_real.reshape(1, b * l, D_MODEL), None, wts["norm1_g"], wts["w_in"], t_in)
        proj3 = proj3.reshape(b, l, W_IN_COLS)
    else:
        pad_rows = front.shape[0]
        proj3 = _inproj(x_real, front, wts["norm1_g"], wts["w_in"], t_in)
    lp = pad_rows + l
    out_a = _pool(proj3, hist16, wts["pool_w"], wts["pool_scale"], t_seq, pos0)
    out_b, hr, hi = _s5(proj3, h0r, h0i, wts["bblk"], wts["cblk"], wts["scan"], wts["ssm_d"], wts["w_glu"], t_seq)
    n = b * l
    if pad_rows == 0:
        x_real, out_a, out_b, projm = (a.reshape(1, n, a.shape[-1]) for a in (x_real, out_a, out_b, proj3))
    else:
        projm = proj3
    x2, xnt = _merge(x_real, out_a, out_b, projm, wts["proj_a"], wts["proj_b"], wts["w_out"],
                     wts["norm2_g"], t_merge, pad_rows // t_merge)
    scores = _query(xnt, wts["peer_wqt"], wts["peer_key1"], wts["peer_key2"])
    n1, e1, r2, e2 = _gates(scores)
    y = _peer(xnt, x2.reshape(n, D_MODEL), wts["peer_u"], wts["peer_vt"], n1, e1, r2, e2,
              wts["final_norm_g"], t_tok, 512)
    new_hist = proj3[:, lp - POOL_HIST:, :W_POOL]
    shp = (b, N_SSM_GROUPS, SSM_STATE)
    return y.reshape(b, l, D_MODEL), new_hist, hr.reshape(shp), hi.reshape(shp)


PROMPT_PAD = 512


def kernel(x_prompt, x_sample, cache_pool, state_ssm_re, state_ssm_im, meta_tokens, norm1_g, w_in, pool_w, pool_scale, ssm_a_re, ssm_a_im, ssm_log_step, ssm_b_re, ssm_b_im, ssm_c_re, ssm_c_im, ssm_d, w_glu, proj_a, proj_b, w_out, norm2_g, peer_wq, peer_key1, peer_key2, peer_u, peer_v, final_norm_g):
    depth = norm1_g.shape[0]
    assert depth == 1
    bp, seq, _ = x_prompt.shape
    bs, dseq, _ = x_sample.shape
    past_len = 1024

    bblk, cblk, scan = _s5_weights(ssm_a_re[0], ssm_a_im[0], ssm_log_step[0], ssm_b_re[0], ssm_b_im[0],
                                   ssm_c_re[0], ssm_c_im[0])
    wts = {
        "norm1_g": norm1_g[0][None], "w_in": w_in[0].astype(BF16),
        "pool_w": pool_w[0].astype(BF16), "pool_scale": pool_scale[0][None],
        "bblk": bblk, "cblk": cblk, "scan": scan, "ssm_d": ssm_d[0][None], "w_glu": w_glu[0].astype(BF16),
        "proj_a": proj_a[0].astype(BF16), "proj_b": proj_b[0].astype(BF16), "w_out": w_out[0].astype(BF16),
        "norm2_g": norm2_g[0][None], "peer_wqt": peer_wq[0].astype(BF16).T,
        "peer_key1": peer_key1[0].astype(BF16), "peer_key2": peer_key2[0].astype(BF16),
        "peer_u": peer_u[0].astype(BF16), "peer_vt": peer_v[0].astype(BF16).T,
        "final_norm_g": final_norm_g[None],
    }

    front = jnp.concatenate([jnp.zeros((PROMPT_PAD - N_META, D_MODEL), x_prompt.dtype), meta_tokens], axis=0)
    zst = jnp.zeros((bp, SSM_SB, SSM_SB_LANES), F32)
    yp, pool_p, re_p, im_p = _group(
        x_prompt, front, jnp.zeros((bp, POOL_HIST + 1, W_POOL), F32), zst, zst, wts,
        pos0=1 - (PROMPT_PAD - N_META), t_in=PROMPT_PAD, t_seq=256, t_merge=256, t_tok=512)

    hist16 = jnp.pad(cache_pool[0], ((0, 0), (1, 0), (0, 0)))
    ys, pool_s, re_s, im_s = _group(
        x_sample, None, hist16,
        state_ssm_re[0].reshape(bs, SSM_SB, SSM_SB_LANES), state_ssm_im[0].reshape(bs, SSM_SB, SSM_SB_LANES),
        wts, pos0=1 + past_len, t_in=512, t_seq=dseq, t_merge=256, t_tok=512)

    return (yp, ys, pool_p[None], re_p[None], im_p[None], pool_s[None], re_s[None], im_s[None])
```

```python
import functools

import jax
import jax.numpy as jnp
from jax import lax
from jax.experimental import pallas as pl
from jax.experimental.pallas import tpu as pltpu

D_MODEL = 2048
N_META = 16
PAST_LEN = 1024
W_POOL = 1024
W_SSM = 1024
POOL_SIZES = (2, 4, 8, 16)
POOL_GROUP = 256
POOL_HIST = 15
N_SSM_GROUPS = 64
SSM_GROUP = 16
SSM_STATE = 64
PEER_HEADS = 8
PEER_TOPK = 16
N_KEYS = 128
N_EXPERTS = N_KEYS * N_KEYS
D_QUERY = 256
D_HALF = 128
W_IN_COLS = W_POOL + W_SSM + 2 * D_MODEL
NORM_EPS = 1e-6

F32 = jnp.float32
BF16 = jnp.bfloat16

SSM_SB = 4
SSM_SB_CH = W_SSM // SSM_SB
SSM_SB_LANES = (N_SSM_GROUPS // SSM_SB) * SSM_STATE
SCAN_ROWS = 8
SCAN_LANES = 256
SCAN_UNROLL = 8
CARRY_LANES = 512

V7X_VMEM_BYTES = 64 * 1024 * 1024
VMEM_LIMIT = V7X_VMEM_BYTES - 8 * 1024 * 1024
MXU_DIM = 256

T_INPROJ = 512
T_SEQ = 256
T_MERGE = 256
INPROJ_COLS = 2048
PEER_CHUNK = 512


def _const_spec(shape):
    nd = len(shape)
    return pl.BlockSpec(shape, lambda *_: (0,) * nd, pipeline_mode=pl.Buffered(1))


def _params(*sem):
    return pltpu.CompilerParams(dimension_semantics=sem, vmem_limit_bytes=VMEM_LIMIT)


def _rms(x, g):
    return x * lax.rsqrt(jnp.mean(x * x, axis=-1, keepdims=True) + NORM_EPS) * g


def _inproj_kernel(x_ref, f_ref, g_ref, w_ref, o_ref, xn_ref, *, front_blocks):
    i = pl.program_id(1)
    first_col = pl.program_id(2) == 0

    @pl.when(first_col & (i >= front_blocks))
    def _():
        xn_ref[...] = _rms(x_ref[0], g_ref[...]).astype(BF16)

    if front_blocks:
        @pl.when(first_col & (i < front_blocks))
        def _():
            xn_ref[...] = _rms(f_ref[...], g_ref[...]).astype(BF16)

    o_ref[0] = jnp.dot(xn_ref[...], w_ref[...], preferred_element_type=F32)


def _inproj(x3, front, g, w_bf16, tm):
    b, l, _ = x3.shape
    cb = INPROJ_COLS
    fb = 0 if front is None else 1
    if front is None:
        front = jnp.zeros((SUBLANE, D_MODEL), F32)
    return pl.pallas_call(
        functools.partial(_inproj_kernel, front_blocks=fb),
        out_shape=jax.ShapeDtypeStruct((b, fb * tm + l, W_IN_COLS), F32),
        grid=(b, fb + l // tm, W_IN_COLS // cb),
        in_specs=[
            pl.BlockSpec((1, tm, D_MODEL), lambda s, i, j: (s, jnp.maximum(i - fb, 0), 0)),
            _const_spec(front.shape),
            _const_spec((1, D_MODEL)),
            pl.BlockSpec((D_MODEL, cb), lambda s, i, j: (0, j)),
        ],
        out_specs=pl.BlockSpec((1, tm, cb), lambda s, i, j: (s, i, j)),
        scratch_shapes=[pltpu.VMEM((tm, D_MODEL), BF16)],
        compiler_params=_params("parallel", "arbitrary", "arbitrary"),
        name="inproj",
    )(x3, front, g, w_bf16)


def _pool_kernel(u_ref, hist_ref, pw_ref, ps_ref, o_ref, ext_ref, *, tp, pos0):
    i = pl.program_id(1)
    hrows = POOL_HIST + 1

    @pl.when(i == 0)
    def _():
        ext_ref[0:hrows, :] = hist_ref[0]

    @pl.when(i > 0)
    def _():
        ext_ref[0:hrows, :] = ext_ref[tp:tp + hrows, :]

    ext_ref[hrows:hrows + tp, :] = u_ref[0]
    row = lax.broadcasted_iota(jnp.int32, (tp, 1), 0)
    avail = i * tp + row + pos0
    for gi, w in enumerate(POOL_SIZES):
        c0 = gi * POOL_GROUP
        u = ext_ref[hrows:hrows + tp, c0:c0 + POOL_GROUP]
        acc = u
        for j in range(1, w):
            acc = acc + ext_ref[hrows - j:hrows - j + tp, c0:c0 + POOL_GROUP]
        cnt = jnp.maximum(jnp.minimum(avail, w), 1).astype(F32)
        pooled = acc / cnt - u
        mixed = jnp.dot(pooled.astype(BF16), pw_ref[gi], preferred_element_type=F32)
        o_ref[0, :, c0:c0 + POOL_GROUP] = (mixed * ps_ref[:, c0:c0 + POOL_GROUP]).astype(o_ref.dtype)


def _pool(proj3, hist16, pw_bf16, pscale, tp, pos0):
    b, l, _ = proj3.shape
    return pl.pallas_call(
        functools.partial(_pool_kernel, tp=tp, pos0=pos0),
        out_shape=jax.ShapeDtypeStruct((b, l, W_POOL), BF16),
        grid=(b, l // tp),
        in_specs=[
            pl.BlockSpec((1, tp, W_POOL), lambda s, i: (s, i, 0)),
            pl.BlockSpec((1, POOL_HIST + 1, W_POOL), lambda s, i: (s, 0, 0)),
            _const_spec((len(POOL_SIZES), POOL_GROUP, POOL_GROUP)),
            _const_spec((1, W_POOL)),
        ],
        out_specs=pl.BlockSpec((1, tp, W_POOL), lambda s, i: (s, i, 0)),
        scratch_shapes=[pltpu.VMEM((POOL_HIST + 1 + tp, W_POOL), F32)],
        compiler_params=_params("parallel", "arbitrary"),
        name="pool",
    )(proj3, hist16, pw_bf16, pscale)


def _s5_kernel(u_ref, h0r_ref, h0i_ref, bblk_ref, cblk_ref, sc_ref, d_ref, wglu_ref,
               o_ref, hr_out, hi_out, bu_ref, xs_ref, y_ref, cr_ref, ci_ref, *, ts):
    i = pl.program_id(1)

    @pl.when(i == 0)
    def _():
        cr_ref[...] = h0r_ref[0]
        ci_ref[...] = h0i_ref[0]

    u = u_ref[0]
    ub = u.astype(BF16)
    for sb in range(SSM_SB):
        bu_ref[...] = jnp.dot(ub[:, sb * SSM_SB_CH:(sb + 1) * SSM_SB_CH], bblk_ref[sb],
                              preferred_element_type=F32)
        for lc in range(SSM_SB_LANES // SCAN_LANES):
            re = slice(lc * SCAN_LANES, (lc + 1) * SCAN_LANES)
            im = slice(SSM_SB_LANES + lc * SCAN_LANES, SSM_SB_LANES + (lc + 1) * SCAN_LANES)
            a1r, a1i, a2r, a2i, a4r, a4i = [sc_ref[sb, k, :, re] for k in range(6)]
            steps = ((a1r, a1i, 1), (a2r, a2i, 2), (a4r, a4i, 4))

            def local_scan(r, _, re=re, im=im, steps=steps):
                rows = pl.ds(pl.multiple_of(r * SCAN_ROWS, SCAN_ROWS), SCAN_ROWS)
                xr = bu_ref[rows, re]
                xi = bu_ref[rows, im]
                for ar, ai, d in steps:
                    sr = pltpu.roll(xr, d, 0)
                    si = pltpu.roll(xi, d, 0)
                    xr, xi = xr + ar * sr - ai * si, xi + ar * si + ai * sr
                xs_ref[rows, re] = xr
                xs_ref[rows, im] = xi
                return 0

            lax.fori_loop(0, ts // SCAN_ROWS, local_scan, 0, unroll=SCAN_UNROLL)
        for lc in range(SSM_SB_LANES // CARRY_LANES):
            re = slice(lc * CARRY_LANES, (lc + 1) * CARRY_LANES)
            im = slice(SSM_SB_LANES + lc * CARRY_LANES, SSM_SB_LANES + (lc + 1) * CARRY_LANES)
            pr = sc_ref[sb, 6, :, re]
            pi = sc_ref[sb, 7, :, re]
            cr0 = jnp.broadcast_to(cr_ref[sb:sb + 1, re], (SCAN_ROWS, CARRY_LANES))
            ci0 = jnp.broadcast_to(ci_ref[sb:sb + 1, re], (SCAN_ROWS, CARRY_LANES))

            def add_carry(r, carry, re=re, im=im, pr=pr, pi=pi):
                cr, ci = carry
                rows = pl.ds(pl.multiple_of(r * SCAN_ROWS, SCAN_ROWS), SCAN_ROWS)
                hr = xs_ref[rows, re] + pr * cr - pi * ci
                hi = xs_ref[rows, im] + pr * ci + pi * cr
                bu_ref[rows, re] = hr
                bu_ref[rows, im] = hi
                last = slice(SCAN_ROWS - 1, SCAN_ROWS)
                return (jnp.broadcast_to(hr[last], (SCAN_ROWS, CARRY_LANES)),
                        jnp.broadcast_to(hi[last], (SCAN_ROWS, CARRY_LANES)))

            cr, ci = lax.fori_loop(0, ts // SCAN_ROWS, add_carry, (cr0, ci0), unroll=2)
            cr_ref[sb:sb + 1, re] = cr[0:1]
            ci_ref[sb:sb + 1, re] = ci[0:1]
        y_ref[:, sb * SSM_SB_CH:(sb + 1) * SSM_SB_CH] = jnp.dot(
            bu_ref[...].astype(BF16), cblk_ref[sb], preferred_element_type=F32)
    y = jax.nn.gelu(y_ref[...] + d_ref[...] * u)
    z = jnp.dot(y.astype(BF16), wglu_ref[...], preferred_element_type=F32)
    o_ref[0] = (y * jax.nn.sigmoid(z)).astype(o_ref.dtype)

    @pl.when(i == pl.num_programs(1) - 1)
    def _():
        hr_out[0] = cr_ref[...]
        hi_out[0] = ci_ref[...]


def _s5(proj3, h0r, h0i, bblk, cblk, scan_consts, d_skip, wglu_bf16, ts):
    b, l, _ = proj3.shape
    st_spec = pl.BlockSpec((1, SSM_SB, SSM_SB_LANES), lambda s, i: (s, 0, 0))
    st_shape = jax.ShapeDtypeStruct((b, SSM_SB, SSM_SB_LANES), F32)
    return pl.pallas_call(
        functools.partial(_s5_kernel, ts=ts),
        out_shape=(jax.ShapeDtypeStruct((b, l, W_SSM), BF16), st_shape, st_shape),
        grid=(b, l // ts),
        in_specs=[
            pl.BlockSpec((1, ts, W_SSM), lambda s, i: (s, i, 1)),
            st_spec, st_spec,
            _const_spec(bblk.shape), _const_spec(cblk.shape), _const_spec(scan_consts.shape),
            _const_spec((1, W_SSM)), _const_spec((W_SSM, W_SSM)),
        ],
        out_specs=(pl.BlockSpec((1, ts, W_SSM), lambda s, i: (s, i, 0)), st_spec, st_spec),
        scratch_shapes=[
            pltpu.VMEM((ts, 2 * SSM_SB_LANES), F32),
            pltpu.VMEM((ts, 2 * SSM_SB_LANES), F32),
            pltpu.VMEM((ts, W_SSM), F32),
            pltpu.VMEM((SSM_SB, SSM_SB_LANES), F32),
            pltpu.VMEM((SSM_SB, SSM_SB_LANES), F32),
        ],
        compiler_params=_params("parallel", "arbitrary"),
        name="s5",
    )(proj3, h0r, h0i, bblk, cblk, scan_consts, d_skip, wglu_bf16)


def _s5_weights(a_re, a_im, log_step, b_re, b_im, c_re, c_im):
    dt = jnp.exp(log_step)[:, None]
    mag = jnp.exp(a_re * dt)
    abr = mag * jnp.cos(a_im * dt)
    abi = mag * jnp.sin(a_im * dt)
    nr, ni = abr - 1.0, abi
    den = a_re * a_re + a_im * a_im
    f_re = (nr * a_re + ni * a_im) / den
    f_im = (ni * a_re - nr * a_im) / den
    bbr = f_re[..., None] * b_re - f_im[..., None] * b_im
    bbi = f_re[..., None] * b_im + f_im[..., None] * b_re
    gl = N_SSM_GROUPS // SSM_SB
    eye = jnp.eye(gl, dtype=F32)

    def bdiag_in(x):
        x = x.reshape(SSM_SB, gl, SSM_STATE, SSM_GROUP)
        return jnp.einsum("sgpc,gh->sgchp", x, eye).reshape(SSM_SB, gl * SSM_GROUP, gl * SSM_STATE)

    def bdiag_out(x):
        x = x.reshape(SSM_SB, gl, SSM_GROUP, SSM_STATE)
        return jnp.einsum("sgcp,gh->sgphc", x, eye).reshape(SSM_SB, gl * SSM_STATE, gl * SSM_GROUP)

    bblk = jnp.concatenate([bdiag_in(bbr), bdiag_in(bbi)], axis=2).astype(BF16)
    cblk = jnp.concatenate([bdiag_out(c_re), -bdiag_out(c_im)], axis=1).astype(BF16)

    ar = abr.reshape(SSM_SB, SSM_SB_LANES)
    ai = abi.reshape(SSM_SB, SSM_SB_LANES)
    pw_r, pw_i = [ar], [ai]
    for _ in range(SCAN_ROWS - 1):
        pr, pi = pw_r[-1], pw_i[-1]
        pw_r.append(pr * ar - pi * ai)
        pw_i.append(pr * ai + pi * ar)
    row = jnp.arange(SCAN_ROWS)[None, :, None]

    def step(d):
        m = (row >= d).astype(F32)
        return pw_r[d - 1][:, None, :] * m, pw_i[d - 1][:, None, :] * m

    s1r, s1i = step(1)
    s2r, s2i = step(2)
    s4r, s4i = step(4)
    car_r = jnp.stack(pw_r, axis=1)
    car_i = jnp.stack(pw_i, axis=1)
    consts = jnp.stack([s1r, s1i, s2r, s2i, s4r, s4i, car_r, car_i], axis=1)
    return bblk, cblk, consts


def _merge_kernel(x_ref, oa_ref, ob_ref, ga_ref, gb_ref, pa_ref, pb_ref, wo_ref, g2_ref,
                  x2_ref, xnt_ref):
    ma = jnp.dot(oa_ref[0], pa_ref[...], preferred_element_type=F32)
    mb = jnp.dot(ob_ref[0], pb_ref[...], preferred_element_type=F32)
    merged = jax.nn.sigmoid(ga_ref[0]) * ma + jax.nn.sigmoid(gb_ref[0]) * mb
    x2 = x_ref[0] + jnp.dot(merged.astype(BF16), wo_ref[...], preferred_element_type=F32)
    x2_ref[0] = x2
    xnt_ref[...] = _rms(x2, g2_ref[...]).T.astype(BF16)


def _merge(x3, oa3, ob3, proj3, pa, pb, wo, g2, tm, pad_blocks):
    b, l, _ = x3.shape
    off = pad_blocks
    nblk = l // tm
    return pl.pallas_call(
        _merge_kernel,
        out_shape=(jax.ShapeDtypeStruct((b, l, D_MODEL), F32), jax.ShapeDtypeStruct((D_MODEL, b * l), BF16)),
        grid=(b, nblk),
        in_specs=[
            pl.BlockSpec((1, tm, D_MODEL), lambda s, i: (s, i, 0)),
            pl.BlockSpec((1, tm, W_POOL), lambda s, i: (s, i + off, 0)),
            pl.BlockSpec((1, tm, W_SSM), lambda s, i: (s, i + off, 0)),
            pl.BlockSpec((1, tm, D_MODEL), lambda s, i: (s, i + off, 1)),
            pl.BlockSpec((1, tm, D_MODEL), lambda s, i: (s, i + off, 2)),
            _const_spec((W_POOL, D_MODEL)), _const_spec((W_SSM, D_MODEL)),
            _const_spec((D_MODEL, D_MODEL)), _const_spec((1, D_MODEL)),
        ],
        out_specs=(pl.BlockSpec((1, tm, D_MODEL), lambda s, i: (s, i, 0)),
                   pl.BlockSpec((D_MODEL, tm), lambda s, i: (0, s * nblk + i))),
        compiler_params=_params("parallel", "parallel"),
        name="merge",
    )(x3, oa3, ob3, proj3, proj3, pa, pb, wo, g2)


LANE = 128
SUBLANE = 8
PACK_ROWS = 16
PEER_TOKENS = 512
KEY_ROWS = PEER_TOKENS // LANE
SLAB_ROWS = N_KEYS * KEY_ROWS
HEAD_PAIR = SUBLANE // KEY_ROWS
SLAB_PAD = SUBLANE


def _slab_shift(kind, head):
    return kind + 2 * (head % HEAD_PAIR)


def _query_kernel(xnt_ref, wqt_ref, k1_ref, k2_ref, s_ref):
    qt = jnp.dot(wqt_ref[...], xnt_ref[...], preferred_element_type=F32).astype(BF16)
    for h in range(PEER_HEADS):
        r0 = h * D_QUERY
        halves = (jnp.dot(k1_ref[...], qt[r0:r0 + D_HALF], preferred_element_type=F32),
                  jnp.dot(k2_ref[...], qt[r0 + D_HALF:r0 + D_QUERY], preferred_element_type=F32))
        for half, s in enumerate(halves):
            for j in range(KEY_ROWS):
                s_ref[h, half, 0, pl.ds(j, N_KEYS, stride=KEY_ROWS), :] = s[:, j * 128:(j + 1) * 128]


def _query(xnt, wqt, k1, k2):
    n = xnt.shape[1]
    nblk = n // PEER_TOKENS
    return pl.pallas_call(
        _query_kernel,
        out_shape=jax.ShapeDtypeStruct((PEER_HEADS, 2, nblk, SLAB_ROWS, 128), F32),
        grid=(nblk,),
        in_specs=[
            pl.BlockSpec((D_MODEL, PEER_TOKENS), lambda i: (0, i)),
            _const_spec((PEER_HEADS * D_QUERY, D_MODEL)),
            _const_spec((N_KEYS, D_HALF)), _const_spec((N_KEYS, D_HALF)),
        ],
        out_specs=pl.BlockSpec((PEER_HEADS, 2, 1, SLAB_ROWS, 128), lambda i: (0, 0, i, 0, 0)),
        compiler_params=_params("parallel"),
        name="query",
    )(xnt, wqt, k1, k2)


def _oddeven_merge_sort_pairs(n):
    pairs = []
    p = 1
    while p < n:
        k = p
        while k >= 1:
            for j in range(k % p, n - k, 2 * k):
                for i in range(min(k, n - j - k)):
                    if (i + j) // (p * 2) == (i + j + k) // (p * 2):
                        pairs.append((i + j, i + j + k))
            k //= 2
        p *= 2
    return pairs


_SORT16_PAIRS = _oddeven_merge_sort_pairs(PEER_TOPK)


def _cmpx(v, i, j):
    a, b = v[i], v[j]
    v[i] = jnp.maximum(a, b)
    v[j] = jnp.minimum(a, b)


def _sort16(v):
    v = list(v)
    for i, j in _SORT16_PAIRS:
        _cmpx(v, i, j)
    return v


def _merge_top16(a, b):
    k = PEER_TOPK
    c = [a[i] if b[k - 1 - i] is None else jnp.maximum(a[i], b[k - 1 - i]) for i in range(k)]
    d = k // 2
    while d >= 1:
        for i in range(k):
            if not i & d:
                _cmpx(c, i, i + d)
        d //= 2
    return c


def _top16_values(rows):
    groups = [_sort16(rows[g:g + PEER_TOPK]) for g in range(0, len(rows), PEER_TOPK)]
    while len(groups) > 1:
        groups = [_merge_top16(groups[g], groups[g + 1]) for g in range(0, len(groups), 2)]
    return groups[0]


def _top16(val):
    rows = val.shape[0]
    rowid = lax.broadcasted_iota(jnp.int32, val.shape, 0).astype(F32)
    kshape = (PEER_TOPK,) + val.shape[1:]
    kid = lax.broadcasted_iota(jnp.int32, kshape, 0)

    def body(r, carry):
        val, rank, tops = carry
        m = jnp.max(val, axis=0, keepdims=True)
        idx = jnp.min(jnp.where(val == m, rowid, float(rows)), axis=0, keepdims=True)
        hit = rowid == idx
        rank = jnp.where(hit, lax.convert_element_type(r, F32), rank)
        val = jnp.where(hit, -jnp.inf, val)
        tops = jnp.where(kid == r, m, tops)
        return val, rank, tops

    init = (val, jnp.full(val.shape, float(PEER_TOPK), F32), jnp.zeros(kshape, F32))
    _, rank, tops = lax.fori_loop(0, PEER_TOPK, body, init)
    return rank, tops


def _outer16(a, b, op):
    return jnp.concatenate([op(a[r:r + 1], b) for r in range(PEER_TOPK)], axis=0)


def _gates_kernel(s_ref, n1_ref, e1_ref, r2_ref, e2_ref):
    k = PEER_TOPK

    def load_key(half, i):
        rows = slice(i * KEY_ROWS, (i + 1) * KEY_ROWS)
        return jnp.concatenate([s_ref[hh, half, 0, rows, :] for hh in range(HEAD_PAIR)], axis=0)

    def store_key(ref, i, val, kind=None):
        for hh in range(HEAD_PAIR):
            shift = 0 if kind is None else _slab_shift(kind, hh)
            rows = slice(i * KEY_ROWS + shift, (i + 1) * KEY_ROWS + shift)
            ref[hh, 0, rows, :] = val[hh * KEY_ROWS:(hh + 1) * KEY_ROWS]

    for ref in (r2_ref, e2_ref):
        for hh in range(HEAD_PAIR):
            ref[hh, 0, 0:SUBLANE, :] = jnp.zeros((SUBLANE, LANE), F32)
            ref[hh, 0, SLAB_ROWS:SLAB_ROWS + SLAB_PAD, :] = jnp.zeros((SLAB_PAD, LANE), F32)
    s1 = [load_key(0, i) for i in range(N_KEYS)]
    s2 = [load_key(1, i) for i in range(N_KEYS)]
    v1 = _top16_values(s1)
    v2 = _top16_values(s2)
    width = [k // (r + 1) for r in range(k)]
    cell = [[v1[r1] + v2[r2] for r2 in range(width[r1])] for r1 in range(k)]
    top = cell[0]
    for r1 in range(1, k):
        top = _merge_top16(top, cell[r1] + [None] * (k - width[r1]))
    tau = top[k - 1]

    def count(vals, thr, cmp):
        n = jnp.zeros_like(thr)
        for v in vals:
            n = n + cmp(v, thr).astype(F32)
        return n

    x1 = [jnp.exp(v - v1[0]) for v in v1]
    x2 = [jnp.exp(v - v2[0]) for v in v2]
    z = jnp.zeros_like(tau)
    nrow = []
    for r1 in range(k):
        n = jnp.zeros_like(tau)
        for r2 in range(width[r1]):
            hit = cell[r1][r2] >= tau
            n = n + hit.astype(F32)
            z = z + jnp.where(hit, x1[r1] * x2[r2], 0.0)
        nrow.append(n)
    ncell = nrow[0]
    for n in nrow[1:]:
        ncell = ncell + n
    ge = lambda a, b: a >= b
    ambiguous = ((count(s1, v1[k - 1], ge) != float(k)) | (count(s2, v2[k - 1], ge) != float(k))
                 | (ncell != float(k)))
    for r1 in range(1, k):
        ambiguous = ambiguous | (v1[r1] + v2[width[r1]] >= tau)
    col = []
    for j in range(1, k + 1):
        c = jnp.full_like(tau, jnp.inf)
        for r1 in range(k // j):
            c = jnp.minimum(c, jnp.where(nrow[r1] >= float(j), v1[r1], jnp.inf))
        col.append(c)
    inv_z = 1.0 / z
    gt = lambda a, b: a > b
    for i in range(N_KEYS):
        store_key(n1_ref, i, count(col, s1[i], lambda c, s: s >= c))
        store_key(r2_ref, i, count(v2, s2[i], gt), 0)
        store_key(e1_ref, i, jnp.where(s1[i] >= v1[k - 1], jnp.exp(s1[i] - v1[0]) * inv_z, 0.0))
        store_key(e2_ref, i, jnp.where(s2[i] >= v2[k - 1], jnp.exp(s2[i] - v2[0]), 0.0), 1)

    @pl.when(jnp.max(ambiguous.astype(F32)) > 0.0)
    def _():
        t1 = jnp.stack(s1, axis=0)
        t2 = jnp.stack(s2, axis=0)
        rank1, w1 = _top16(t1)
        rank2, w2 = _top16(t2)
        crank, _ = _top16(_outer16(w1, w2, jnp.add))
        sel = (crank < float(k)).astype(F32)
        y1 = jnp.exp(w1 - w1[0:1])
        y2 = jnp.exp(w2 - w2[0:1])
        zz = jnp.sum(sel * _outer16(y1, y2, jnp.multiply), axis=0, keepdims=True)
        n1 = jnp.zeros_like(t1)
        for r in range(k):
            cnt = jnp.sum(sel[r * k:(r + 1) * k], axis=0, keepdims=True)
            n1 = jnp.where(rank1 == float(r), cnt, n1)
        g1 = jnp.exp(t1 - w1[0:1]) / zz
        g2 = jnp.exp(t2 - w2[0:1])
        for i in range(N_KEYS):
            store_key(n1_ref, i, n1[i])
            store_key(r2_ref, i, rank2[i], 0)
            store_key(e1_ref, i, g1[i])
            store_key(e2_ref, i, g2[i], 1)


def _gates(scores):
    nblk = scores.shape[2]
    spec = pl.BlockSpec((HEAD_PAIR, 1, SLAB_ROWS, LANE), lambda i, g: (g, i, 0, 0))
    shp = jax.ShapeDtypeStruct((PEER_HEADS, nblk, SLAB_ROWS, LANE), F32)
    pspec = pl.BlockSpec((HEAD_PAIR, 1, SLAB_ROWS + SLAB_PAD, LANE), lambda i, g: (g, i, 0, 0))
    pshp = jax.ShapeDtypeStruct((PEER_HEADS, nblk, SLAB_ROWS + SLAB_PAD, LANE), F32)
    return pl.pallas_call(
        _gates_kernel,
        out_shape=(shp, shp, pshp, pshp),
        grid=(nblk, PEER_HEADS // HEAD_PAIR),
        in_specs=[pl.BlockSpec((HEAD_PAIR, 2, 1, SLAB_ROWS, LANE), lambda i, g: (g, 0, i, 0, 0))],
        out_specs=(spec, spec, pspec, pspec),
        compiler_params=_params("parallel", "parallel"),
        name="gates",
    )(scores)


def _peer_kernel(xnt_ref, x2_ref, u_ref, vt_ref, n1_ref, e1_ref, r2_ref, e2_ref, gf_ref,
                 o_ref, acc_ref, w_ref, coef_ref, *, ec, t):
    c = pl.program_id(1)

    @pl.when(c == 0)
    def _():
        coef_ref[...] = jnp.zeros((ec, t), BF16)

    kt = MXU_DIM
    for lt in range(t // LANE):
        lanes = slice(lt * LANE, (lt + 1) * LANE)
        for j in range(ec // N_KEYS):
            krow = slice(j * KEY_ROWS + lt, j * KEY_ROWS + lt + 1)
            n1b = [jnp.broadcast_to(n1_ref[h, 0, krow, :], (SUBLANE, LANE)) for h in range(PEER_HEADS)]
            e1b = [jnp.broadcast_to(e1_ref[h, 0, krow, :], (SUBLANE, LANE)) for h in range(PEER_HEADS)]
            for ib in range(N_KEYS // SUBLANE):
                row0 = ib * SUBLANE * KEY_ROWS + lt
                rows = lambda kind, h: pl.ds(row0 + _slab_shift(kind, h), SUBLANE, stride=KEY_ROWS)
                w = jnp.zeros((SUBLANE, LANE), F32)
                for h in range(PEER_HEADS):
                    hit = r2_ref[h, 0, rows(0, h), :] < n1b[h]
                    w = w + jnp.where(hit, e1b[h] * e2_ref[h, 0, rows(1, h), :], 0.0)
                w_ref[j * N_KEYS + ib * SUBLANE:j * N_KEYS + (ib + 1) * SUBLANE, lanes] = w
    prev = jnp.where(c == 0, 0.0, acc_ref[...])
    acc_ref[...] = prev + jnp.dot(vt_ref[:, 0:kt], coef_ref[0:kt, :], preferred_element_type=F32)
    for k0 in range(kt, ec, kt):
        acc_ref[...] += jnp.dot(vt_ref[:, k0:k0 + kt], coef_ref[k0:k0 + kt, :], preferred_element_type=F32)
    act = jax.nn.gelu(jnp.dot(u_ref[...], xnt_ref[...], preferred_element_type=F32))
    coef_ref[...] = (w_ref[...] * act).astype(BF16)

    @pl.when(c == pl.num_programs(1) - 1)
    def _():
        o_ref[...] = _rms(x2_ref[...] + acc_ref[...].T, gf_ref[...])


def _peer(xnt, x2, u_bf16, vt_bf16, n1, e1, r2, e2, gf, t, ec):
    n = xnt.shape[1]
    nc = N_EXPERTS // ec
    assert t == PEER_TOKENS
    kc = ec // N_KEYS
    cur = lambda c: jnp.minimum(c, nc - 1)
    prv = lambda c: jnp.maximum(c - 1, 0)
    cspec = pl.BlockSpec((PEER_HEADS, 1, kc * KEY_ROWS, LANE), lambda b, c: (0, b, cur(c), 0))
    gspec = pl.BlockSpec((PEER_HEADS, 1, SLAB_ROWS + SLAB_PAD, LANE), lambda b, c: (0, b, 0, 0))
    return pl.pallas_call(
        functools.partial(_peer_kernel, ec=ec, t=t),
        out_shape=jax.ShapeDtypeStruct((n, D_MODEL), F32),
        grid=(n // t, nc + 1),
        in_specs=[
            pl.BlockSpec((D_MODEL, t), lambda b, c: (0, b)),
            pl.BlockSpec((t, D_MODEL), lambda b, c: (b, 0)),
            pl.BlockSpec((ec, D_MODEL), lambda b, c: (cur(c), 0)),
            pl.BlockSpec((D_MODEL, ec), lambda b, c: (0, prv(c))),
            cspec, cspec, gspec, gspec,
            _const_spec((1, D_MODEL)),
        ],
        out_specs=pl.BlockSpec((t, D_MODEL), lambda b, c: (b, 0)),
        scratch_shapes=[
            pltpu.VMEM((D_MODEL, t), F32),
            pltpu.VMEM((ec, t), F32),
            pltpu.VMEM((ec, t), BF16),
        ],
        compiler_params=_params("parallel", "arbitrary"),
        name="peer",
    )(xnt, x2, u_bf16, vt_bf16, n1, e1, r2, e2, gf)


def _group(x_real, front, hist16, h0r, h0i, wts, *, pos0, t_seq):
    b, l, _ = x_real.shape
    if front is None:
        pad_rows = 0
        proj3 = _inproj(x_real.reshape(1, b * l, D_MODEL), None, wts["norm1_g"], wts["w_in"], T_INPROJ)
        proj3 = proj3.reshape(b, l, W_IN_COLS)
    else:
        pad_rows = front.shape[0]
        proj3 = _inproj(x_real, front, wts["norm1_g"], wts["w_in"], T_INPROJ)
    lp = pad_rows + l
    out_a = _pool(proj3, hist16, wts["pool_w"], wts["pool_scale"], t_seq, pos0)
    out_b, hr, hi = _s5(proj3, h0r, h0i, wts["bblk"], wts["cblk"], wts["scan"], wts["ssm_d"], wts["w_glu"], t_seq)
    n = b * l
    if pad_rows == 0:
        x_real, out_a, out_b, projm = (a.reshape(1, n, a.shape[-1]) for a in (x_real, out_a, out_b, proj3))
    else:
        projm = proj3
    x2, xnt = _merge(x_real, out_a, out_b, projm, wts["proj_a"], wts["proj_b"], wts["w_out"],
                     wts["norm2_g"], T_MERGE, pad_rows // T_MERGE)
    scores = _query(xnt, wts["peer_wqt"], wts["peer_key1"], wts["peer_key2"])
    n1, e1, r2, e2 = _gates(scores)
    y = _peer(xnt, x2.reshape(n, D_MODEL), wts["peer_u"], wts["peer_vt"], n1, e1, r2, e2,
              wts["final_norm_g"], PEER_TOKENS, PEER_CHUNK)
    new_hist = proj3[:, lp - POOL_HIST:, :W_POOL]
    shp = (b, N_SSM_GROUPS, SSM_STATE)
    return y.reshape(b, l, D_MODEL), new_hist, hr.reshape(shp), hi.reshape(shp)


def kernel(x_prompt, x_sample, cache_pool, state_ssm_re, state_ssm_im, meta_tokens, norm1_g, w_in, pool_w, pool_scale, ssm_a_re, ssm_a_im, ssm_log_step, ssm_b_re, ssm_b_im, ssm_c_re, ssm_c_im, ssm_d, w_glu, proj_a, proj_b, w_out, norm2_g, peer_wq, peer_key1, peer_key2, peer_u, peer_v, final_norm_g):
    depth = norm1_g.shape[0]
    assert depth == 1
    bp, seq, _ = x_prompt.shape
    bs, dseq, _ = x_sample.shape

    bblk, cblk, scan = _s5_weights(ssm_a_re[0], ssm_a_im[0], ssm_log_step[0], ssm_b_re[0], ssm_b_im[0],
                                   ssm_c_re[0], ssm_c_im[0])
    wts = {
        "norm1_g": norm1_g[0][None], "w_in": w_in[0].astype(BF16),
        "pool_w": pool_w[0].astype(BF16), "pool_scale": pool_scale[0][None],
        "bblk": bblk, "cblk": cblk, "scan": scan, "ssm_d": ssm_d[0][None], "w_glu": w_glu[0].astype(BF16),
        "proj_a": proj_a[0].astype(BF16), "proj_b": proj_b[0].astype(BF16), "w_out": w_out[0].astype(BF16),
        "norm2_g": norm2_g[0][None], "peer_wqt": peer_wq[0].astype(BF16).T,
        "peer_key1": peer_key1[0].astype(BF16), "peer_key2": peer_key2[0].astype(BF16),
        "peer_u": peer_u[0].astype(BF16), "peer_vt": peer_v[0].astype(BF16).T,
        "final_norm_g": final_norm_g[None],
    }

    front = jnp.concatenate([jnp.zeros((T_INPROJ - N_META, D_MODEL), x_prompt.dtype), meta_tokens], axis=0)
    zst = jnp.zeros((bp, SSM_SB, SSM_SB_LANES), F32)
    yp, pool_p, re_p, im_p = _group(
        x_prompt, front, jnp.zeros((bp, POOL_HIST + 1, W_POOL), F32), zst, zst, wts,
        pos0=1 - (T_INPROJ - N_META), t_seq=T_SEQ)

    hist16 = jnp.pad(cache_pool[0], ((0, 0), (1, 0), (0, 0)))
    ys, pool_s, re_s, im_s = _group(
        x_sample, None, hist16,
        state_ssm_re[0].reshape(bs, SSM_SB, SSM_SB_LANES), state_ssm_im[0].reshape(bs, SSM_SB, SSM_SB_LANES),
        wts, pos0=1 + PAST_LEN, t_seq=dseq)

    return (yp, ys, pool_p[None], re_p[None], im_p[None], pool_s[None], re_s[None], im_s[None])
```

```python
import functools

import jax
import jax.numpy as jnp
from jax import lax
from jax.experimental import pallas as pl
from jax.experimental.pallas import tpu as pltpu

D_MODEL = 2048
N_META = 16
PAST_LEN = 1024
W_POOL = 1024
W_SSM = 1024
POOL_SIZES = (2, 4, 8, 16)
POOL_GROUP = 256
POOL_HIST = 15
N_SSM_GROUPS = 64
SSM_GROUP = 16
SSM_STATE = 64
PEER_HEADS = 8
PEER_TOPK = 16
N_KEYS = 128
N_EXPERTS = N_KEYS * N_KEYS
D_QUERY = 256
D_HALF = 128
W_IN_COLS = W_POOL + W_SSM + 2 * D_MODEL
NORM_EPS = 1e-6

F32 = jnp.float32
BF16 = jnp.bfloat16

SSM_SB = 4
SSM_SB_CH = W_SSM // SSM_SB
SSM_SB_LANES = (N_SSM_GROUPS // SSM_SB) * SSM_STATE
SCAN_ROWS = 8
SCAN_LANES = 256
SCAN_UNROLL = 8
CARRY_LANES = 512

V7X_VMEM_BYTES = 64 * 1024 * 1024
VMEM_LIMIT = V7X_VMEM_BYTES - 8 * 1024 * 1024
MXU_DIM = 256

T_INPROJ = 512
T_SEQ = 256
T_MERGE = 256
INPROJ_COLS = 2048
PEER_CHUNK = 512


def _const_spec(shape):
    nd = len(shape)
    return pl.BlockSpec(shape, lambda *_: (0,) * nd, pipeline_mode=pl.Buffered(1))


def _params(*sem):
    return pltpu.CompilerParams(dimension_semantics=sem, vmem_limit_bytes=VMEM_LIMIT)


def _rms(x, g):
    return x * lax.rsqrt(jnp.mean(x * x, axis=-1, keepdims=True) + NORM_EPS) * g


def _inproj_kernel(x_ref, f_ref, g_ref, w_ref, o_ref, xn_ref, *, front_blocks):
    i = pl.program_id(1)
    first_col = pl.program_id(2) == 0

    @pl.when(first_col & (i >= front_blocks))
    def _():
        xn_ref[...] = _rms(x_ref[0], g_ref[...]).astype(BF16)

    if front_blocks:
        @pl.when(first_col & (i < front_blocks))
        def _():
            xn_ref[...] = _rms(f_ref[...], g_ref[...]).astype(BF16)

    o_ref[0] = jnp.dot(xn_ref[...], w_ref[...], preferred_element_type=F32)


def _inproj(x3, front, g, w_bf16, tm):
    b, l, _ = x3.shape
    cb = INPROJ_COLS
    fb = 0 if front is None else 1
    if front is None:
        front = jnp.zeros((SUBLANE, D_MODEL), F32)
    return pl.pallas_call(
        functools.partial(_inproj_kernel, front_blocks=fb),
        out_shape=jax.ShapeDtypeStruct((b, fb * tm + l, W_IN_COLS), F32),
        grid=(b, fb + l // tm, W_IN_COLS // cb),
        in_specs=[
            pl.BlockSpec((1, tm, D_MODEL), lambda s, i, j: (s, jnp.maximum(i - fb, 0), 0)),
            _const_spec(front.shape),
            _const_spec((1, D_MODEL)),
            pl.BlockSpec((D_MODEL, cb), lambda s, i, j: (0, j)),
        ],
        out_specs=pl.BlockSpec((1, tm, cb), lambda s, i, j: (s, i, j)),
        scratch_shapes=[pltpu.VMEM((tm, D_MODEL), BF16)],
        compiler_params=_params("parallel", "arbitrary", "arbitrary"),
        name="inproj",
    )(x3, front, g, w_bf16)


def _pool_kernel(u_ref, hist_ref, pw_ref, ps_ref, o_ref, ext_ref, *, tp, pos0):
    i = pl.program_id(1)
    hrows = POOL_HIST + 1

    @pl.when(i == 0)
    def _():
        ext_ref[0:hrows, :] = hist_ref[0]

    @pl.when(i > 0)
    def _():
        ext_ref[0:hrows, :] = ext_ref[tp:tp + hrows, :]

    ext_ref[hrows:hrows + tp, :] = u_ref[0]
    row = lax.broadcasted_iota(jnp.int32, (tp, 1), 0)
    avail = i * tp + row + pos0
    for gi, w in enumerate(POOL_SIZES):
        c0 = gi * POOL_GROUP
        u = ext_ref[hrows:hrows + tp, c0:c0 + POOL_GROUP]
        acc = u
        for j in range(1, w):
            acc = acc + ext_ref[hrows - j:hrows - j + tp, c0:c0 + POOL_GROUP]
        cnt = jnp.maximum(jnp.minimum(avail, w), 1).astype(F32)
        pooled = acc / cnt - u
        mixed = jnp.dot(pooled.astype(BF16), pw_ref[gi], preferred_element_type=F32)
        o_ref[0, :, c0:c0 + POOL_GROUP] = (mixed * ps_ref[:, c0:c0 + POOL_GROUP]).astype(o_ref.dtype)


def _pool(proj3, hist16, pw_bf16, pscale, tp, pos0):
    b, l, _ = proj3.shape
    return pl.pallas_call(
        functools.partial(_pool_kernel, tp=tp, pos0=pos0),
        out_shape=jax.ShapeDtypeStruct((b, l, W_POOL), BF16),
        grid=(b, l // tp),
        in_specs=[
            pl.BlockSpec((1, tp, W_POOL), lambda s, i: (s, i, 0)),
            pl.BlockSpec((1, POOL_HIST + 1, W_POOL), lambda s, i: (s, 0, 0)),
            _const_spec((len(POOL_SIZES), POOL_GROUP, POOL_GROUP)),
            _const_spec((1, W_POOL)),
        ],
        out_specs=pl.BlockSpec((1, tp, W_POOL), lambda s, i: (s, i, 0)),
        scratch_shapes=[pltpu.VMEM((POOL_HIST + 1 + tp, W_POOL), F32)],
        compiler_params=_params("parallel", "arbitrary"),
        name="pool",
    )(proj3, hist16, pw_bf16, pscale)


def _s5_kernel(u_ref, h0r_ref, h0i_ref, bblk_ref, cblk_ref, sc_ref, d_ref, wglu_ref,
               o_ref, hr_out, hi_out, bu_ref, xs_ref, y_ref, cr_ref, ci_ref, *, ts):
    i = pl.program_id(1)

    @pl.when(i == 0)
    def _():
        cr_ref[...] = h0r_ref[0]
        ci_ref[...] = h0i_ref[0]

    u = u_ref[0]
    ub = u.astype(BF16)
    for sb in range(SSM_SB):
        bu_ref[...] = jnp.dot(ub[:, sb * SSM_SB_CH:(sb + 1) * SSM_SB_CH], bblk_ref[sb],
                              preferred_element_type=F32)
        for lc in range(SSM_SB_LANES // SCAN_LANES):
            re = slice(lc * SCAN_LANES, (lc + 1) * SCAN_LANES)
            im = slice(SSM_SB_LANES + lc * SCAN_LANES, SSM_SB_LANES + (lc + 1) * SCAN_LANES)
            a1r, a1i, a2r, a2i, a4r, a4i = [sc_ref[sb, k, :, re] for k in range(6)]
            steps = ((a1r, a1i, 1), (a2r, a2i, 2), (a4r, a4i, 4))

            def local_scan(r, _, re=re, im=im, steps=steps):
                rows = pl.ds(pl.multiple_of(r * SCAN_ROWS, SCAN_ROWS), SCAN_ROWS)
                xr = bu_ref[rows, re]
                xi = bu_ref[rows, im]
                for ar, ai, d in steps:
                    sr = pltpu.roll(xr, d, 0)
                    si = pltpu.roll(xi, d, 0)
                    xr, xi = xr + ar * sr - ai * si, xi + ar * si + ai * sr
                xs_ref[rows, re] = xr
                xs_ref[rows, im] = xi
                return 0

            lax.fori_loop(0, ts // SCAN_ROWS, local_scan, 0, unroll=SCAN_UNROLL)
        for lc in range(SSM_SB_LANES // CARRY_LANES):
            re = slice(lc * CARRY_LANES, (lc + 1) * CARRY_LANES)
            im = slice(SSM_SB_LANES + lc * CARRY_LANES, SSM_SB_LANES + (lc + 1) * CARRY_LANES)
            pr = sc_ref[sb, 6, :, re]
            pi = sc_ref[sb, 7, :, re]
            cr0 = jnp.broadcast_to(cr_ref[sb:sb + 1, re], (SCAN_ROWS, CARRY_LANES))
            ci0 = jnp.broadcast_to(ci_ref[sb:sb + 1, re], (SCAN_ROWS, CARRY_LANES))

            def add_carry(r, carry, re=re, im=im, pr=pr, pi=pi):
                cr, ci = carry
                rows = pl.ds(pl.multiple_of(r * SCAN_ROWS, SCAN_ROWS), SCAN_ROWS)
                hr = xs_ref[rows, re] + pr * cr - pi * ci
                hi = xs_ref[rows, im] + pr * ci + pi * cr
                bu_ref[rows, re] = hr
                bu_ref[rows, im] = hi
                last = slice(SCAN_ROWS - 1, SCAN_ROWS)
                return (jnp.broadcast_to(hr[last], (SCAN_ROWS, CARRY_LANES)),
                        jnp.broadcast_to(hi[last], (SCAN_ROWS, CARRY_LANES)))

            cr, ci = lax.fori_loop(0, ts // SCAN_ROWS, add_carry, (cr0, ci0), unroll=2)
            cr_ref[sb:sb + 1, re] = cr[0:1]
            ci_ref[sb:sb + 1, re] = ci[0:1]
        y_ref[:, sb * SSM_SB_CH:(sb + 1) * SSM_SB_CH] = jnp.dot(
            bu_ref[...].astype(BF16), cblk_ref[sb], preferred_element_type=F32)
    y = jax.nn.gelu(y_ref[...] + d_ref[...] * u)
    z = jnp.dot(y.astype(BF16), wglu_ref[...], preferred_element_type=F32)
    o_ref[0] = (y * jax.nn.sigmoid(z)).astype(o_ref.dtype)

    @pl.when(i == pl.num_programs(1) - 1)
    def _():
        hr_out[0] = cr_ref[...]
        hi_out[0] = ci_ref[...]


def _s5(proj3, h0r, h0i, bblk, cblk, scan_consts, d_skip, wglu_bf16, ts):
    b, l, _ = proj3.shape
    st_spec = pl.BlockSpec((1, SSM_SB, SSM_SB_LANES), lambda s, i: (s, 0, 0))
    st_shape = jax.ShapeDtypeStruct((b, SSM_SB, SSM_SB_LANES), F32)
    return pl.pallas_call(
        functools.partial(_s5_kernel, ts=ts),
        out_shape=(jax.ShapeDtypeStruct((b, l, W_SSM), BF16), st_shape, st_shape),
        grid=(b, l // ts),
        in_specs=[
            pl.BlockSpec((1, ts, W_SSM), lambda s, i: (s, i, 1)),
            st_spec, st_spec,
            _const_spec(bblk.shape), _const_spec(cblk.shape), _const_spec(scan_consts.shape),
            _const_spec((1, W_SSM)), _const_spec((W_SSM, W_SSM)),
        ],
        out_specs=(pl.BlockSpec((1, ts, W_SSM), lambda s, i: (s, i, 0)), st_spec, st_spec),
        scratch_shapes=[
            pltpu.VMEM((ts, 2 * SSM_SB_LANES), F32),
            pltpu.VMEM((ts, 2 * SSM_SB_LANES), F32),
            pltpu.VMEM((ts, W_SSM), F32),
            pltpu.VMEM((SSM_SB, SSM_SB_LANES), F32),
            pltpu.VMEM((SSM_SB, SSM_SB_LANES), F32),
        ],
        compiler_params=_params("parallel", "arbitrary"),
        name="s5",
    )(proj3, h0r, h0i, bblk, cblk, scan_consts, d_skip, wglu_bf16)


def _s5_weights(a_re, a_im, log_step, b_re, b_im, c_re, c_im):
    dt = jnp.exp(log_step)[:, None]
    mag = jnp.exp(a_re * dt)
    abr = mag * jnp.cos(a_im * dt)
    abi = mag * jnp.sin(a_im * dt)
    nr, ni = abr - 1.0, abi
    den = a_re * a_re + a_im * a_im
    f_re = (nr * a_re + ni * a_im) / den
    f_im = (ni * a_re - nr * a_im) / den
    bbr = f_re[..., None] * b_re - f_im[..., None] * b_im
    bbi = f_re[..., None] * b_im + f_im[..., None] * b_re
    gl = N_SSM_GROUPS // SSM_SB
    eye = jnp.eye(gl, dtype=F32)

    def bdiag_in(x):
        x = x.reshape(SSM_SB, gl, SSM_STATE, SSM_GROUP)
        return jnp.einsum("sgpc,gh->sgchp", x, eye).reshape(SSM_SB, gl * SSM_GROUP, gl * SSM_STATE)

    def bdiag_out(x):
        x = x.reshape(SSM_SB, gl, SSM_GROUP, SSM_STATE)
        return jnp.einsum("sgcp,gh->sgphc", x, eye).reshape(SSM_SB, gl * SSM_STATE, gl * SSM_GROUP)

    bblk = jnp.concatenate([bdiag_in(bbr), bdiag_in(bbi)], axis=2).astype(BF16)
    cblk = jnp.concatenate([bdiag_out(c_re), -bdiag_out(c_im)], axis=1).astype(BF16)

    ar = abr.reshape(SSM_SB, SSM_SB_LANES)
    ai = abi.reshape(SSM_SB, SSM_SB_LANES)
    pw_r, pw_i = [ar], [ai]
    for _ in range(SCAN_ROWS - 1):
        pr, pi = pw_r[-1], pw_i[-1]
        pw_r.append(pr * ar - pi * ai)
        pw_i.append(pr * ai + pi * ar)
    row = jnp.arange(SCAN_ROWS)[None, :, None]

    def step(d):
        m = (row >= d).astype(F32)
        return pw_r[d - 1][:, None, :] * m, pw_i[d - 1][:, None, :] * m

    s1r, s1i = step(1)
    s2r, s2i = step(2)
    s4r, s4i = step(4)
    car_r = jnp.stack(pw_r, axis=1)
    car_i = jnp.stack(pw_i, axis=1)
    consts = jnp.stack([s1r, s1i, s2r, s2i, s4r, s4i, car_r, car_i], axis=1)
    return bblk, cblk, consts


def _merge_kernel(x_ref, oa_ref, ob_ref, ga_ref, gb_ref, pa_ref, pb_ref, wo_ref, g2_ref,
                  x2_ref, xnt_ref):
    ma = jnp.dot(oa_ref[0], pa_ref[...], preferred_element_type=F32)
    mb = jnp.dot(ob_ref[0], pb_ref[...], preferred_element_type=F32)
    merged = jax.nn.sigmoid(ga_ref[0]) * ma + jax.nn.sigmoid(gb_ref[0]) * mb
    x2 = x_ref[0] + jnp.dot(merged.astype(BF16), wo_ref[...], preferred_element_type=F32)
    x2_ref[0] = x2
    xnt_ref[...] = _rms(x2, g2_ref[...]).T.astype(BF16)


def _merge(x3, oa3, ob3, proj3, pa, pb, wo, g2, tm, pad_blocks):
    b, l, _ = x3.shape
    off = pad_blocks
    nblk = l // tm
    return pl.pallas_call(
        _merge_kernel,
        out_shape=(jax.ShapeDtypeStruct((b, l, D_MODEL), F32), jax.ShapeDtypeStruct((D_MODEL, b * l), BF16)),
        grid=(b, nblk),
        in_specs=[
            pl.BlockSpec((1, tm, D_MODEL), lambda s, i: (s, i, 0)),
            pl.BlockSpec((1, tm, W_POOL), lambda s, i: (s, i + off, 0)),
            pl.BlockSpec((1, tm, W_SSM), lambda s, i: (s, i + off, 0)),
            pl.BlockSpec((1, tm, D_MODEL), lambda s, i: (s, i + off, 1)),
            pl.BlockSpec((1, tm, D_MODEL), lambda s, i: (s, i + off, 2)),
            _const_spec((W_POOL, D_MODEL)), _const_spec((W_SSM, D_MODEL)),
            _const_spec((D_MODEL, D_MODEL)), _const_spec((1, D_MODEL)),
        ],
        out_specs=(pl.BlockSpec((1, tm, D_MODEL), lambda s, i: (s, i, 0)),
                   pl.BlockSpec((D_MODEL, tm), lambda s, i: (0, s * nblk + i))),
        compiler_params=_params("parallel", "parallel"),
        name="merge",
    )(x3, oa3, ob3, proj3, proj3, pa, pb, wo, g2)


LANE = 128
SUBLANE = 8
PACK_ROWS = 16
PEER_TOKENS = 512
KEY_ROWS = PEER_TOKENS // LANE
SLAB_ROWS = N_KEYS * KEY_ROWS
HEAD_PAIR = SUBLANE // KEY_ROWS
SLAB_PAD = SUBLANE


def _slab_shift(kind, head):
    return kind + 2 * (head % HEAD_PAIR)


def _query_kernel(xnt_ref, wqt_ref, k1_ref, k2_ref, s_ref):
    qt = jnp.dot(wqt_ref[...], xnt_ref[...], preferred_element_type=F32).astype(BF16)
    for h in range(PEER_HEADS):
        r0 = h * D_QUERY
        halves = (jnp.dot(k1_ref[...], qt[r0:r0 + D_HALF], preferred_element_type=F32),
                  jnp.dot(k2_ref[...], qt[r0 + D_HALF:r0 + D_QUERY], preferred_element_type=F32))
        for half, s in enumerate(halves):
            for j in range(KEY_ROWS):
                s_ref[h, half, 0, pl.ds(j, N_KEYS, stride=KEY_ROWS), :] = s[:, j * 128:(j + 1) * 128]


def _query(xnt, wqt, k1, k2):
    n = xnt.shape[1]
    nblk = n // PEER_TOKENS
    return pl.pallas_call(
        _query_kernel,
        out_shape=jax.ShapeDtypeStruct((PEER_HEADS, 2, nblk, SLAB_ROWS, 128), F32),
        grid=(nblk,),
        in_specs=[
            pl.BlockSpec((D_MODEL, PEER_TOKENS), lambda i: (0, i)),
            _const_spec((PEER_HEADS * D_QUERY, D_MODEL)),
            _const_spec((N_KEYS, D_HALF)), _const_spec((N_KEYS, D_HALF)),
        ],
        out_specs=pl.BlockSpec((PEER_HEADS, 2, 1, SLAB_ROWS, 128), lambda i: (0, 0, i, 0, 0)),
        compiler_params=_params("parallel"),
        name="query",
    )(xnt, wqt, k1, k2)


def _oddeven_merge_sort_pairs(n):
    pairs = []
    p = 1
    while p < n:
        k = p
        while k >= 1:
            for j in range(k % p, n - k, 2 * k):
                for i in range(min(k, n - j - k)):
                    if (i + j) // (p * 2) == (i + j + k) // (p * 2):
                        pairs.append((i + j, i + j + k))
            k //= 2
        p *= 2
    return pairs


_SORT16_PAIRS = _oddeven_merge_sort_pairs(PEER_TOPK)


def _cmpx(v, i, j):
    a, b = v[i], v[j]
    v[i] = jnp.maximum(a, b)
    v[j] = jnp.minimum(a, b)


def _sort16(v):
    v = list(v)
    for i, j in _SORT16_PAIRS:
        _cmpx(v, i, j)
    return v


def _merge_top16(a, b):
    k = PEER_TOPK
    c = [a[i] if b[k - 1 - i] is None else jnp.maximum(a[i], b[k - 1 - i]) for i in range(k)]
    d = k // 2
    while d >= 1:
        for i in range(k):
            if not i & d:
                _cmpx(c, i, i + d)
        d //= 2
    return c


def _top16_values(rows):
    groups = [_sort16(rows[g:g + PEER_TOPK]) for g in range(0, len(rows), PEER_TOPK)]
    while len(groups) > 1:
        groups = [_merge_top16(groups[g], groups[g + 1]) for g in range(0, len(groups), 2)]
    return groups[0]


def _top16(val):
    rows = val.shape[0]
    rowid = lax.broadcasted_iota(jnp.int32, val.shape, 0).astype(F32)
    kshape = (PEER_TOPK,) + val.shape[1:]
    kid = lax.broadcasted_iota(jnp.int32, kshape, 0)

    def body(r, carry):
        val, rank, tops = carry
        m = jnp.max(val, axis=0, keepdims=True)
        idx = jnp.min(jnp.where(val == m, rowid, float(rows)), axis=0, keepdims=True)
        hit = rowid == idx
        rank = jnp.where(hit, lax.convert_element_type(r, F32), rank)
        val = jnp.where(hit, -jnp.inf, val)
        tops = jnp.where(kid == r, m, tops)
        return val, rank, tops

    init = (val, jnp.full(val.shape, float(PEER_TOPK), F32), jnp.zeros(kshape, F32))
    _, rank, tops = lax.fori_loop(0, PEER_TOPK, body, init)
    return rank, tops


def _outer16(a, b, op):
    return jnp.concatenate([op(a[r:r + 1], b) for r in range(PEER_TOPK)], axis=0)


def _gates_kernel(s_ref, n1_ref, e1_ref, r2_ref, e2_ref):
    k = PEER_TOPK

    def load_key(half, i):
        rows = slice(i * KEY_ROWS, (i + 1) * KEY_ROWS)
        return jnp.concatenate([s_ref[hh, half, 0, rows, :] for hh in range(HEAD_PAIR)], axis=0)

    def store_key(ref, i, val, kind=None):
        for hh in range(HEAD_PAIR):
            shift = 0 if kind is None else _slab_shift(kind, hh)
            rows = slice(i * KEY_ROWS + shift, (i + 1) * KEY_ROWS + shift)
            ref[hh, 0, rows, :] = val[hh * KEY_ROWS:(hh + 1) * KEY_ROWS]

    for ref in (r2_ref, e2_ref):
        for hh in range(HEAD_PAIR):
            ref[hh, 0, 0:SUBLANE, :] = jnp.zeros((SUBLANE, LANE), F32)
            ref[hh, 0, SLAB_ROWS:SLAB_ROWS + SLAB_PAD, :] = jnp.zeros((SLAB_PAD, LANE), F32)
    s1 = [load_key(0, i) for i in range(N_KEYS)]
    s2 = [load_key(1, i) for i in range(N_KEYS)]
    v1 = _top16_values(s1)
    v2 = _top16_values(s2)
    width = [k // (r + 1) for r in range(k)]
    cell = [[v1[r1] + v2[r2] for r2 in range(width[r1])] for r1 in range(k)]
    top = cell[0]
    for r1 in range(1, k):
        top = _merge_top16(top, cell[r1] + [None] * (k - width[r1]))
    tau = top[k - 1]

    def count(vals, thr, cmp):
        n = jnp.zeros_like(thr)
        for v in vals:
            n = n + cmp(v, thr).astype(F32)
        return n

    x1 = [jnp.exp(v - v1[0]) for v in v1]
    x2 = [jnp.exp(v - v2[0]) for v in v2]
    z = jnp.zeros_like(tau)
    nrow = []
    for r1 in range(k):
        n = jnp.zeros_like(tau)
        for r2 in range(width[r1]):
            hit = cell[r1][r2] >= tau
            n = n + hit.astype(F32)
            z = z + jnp.where(hit, x1[r1] * x2[r2], 0.0)
        nrow.append(n)
    ncell = nrow[0]
    for n in nrow[1:]:
        ncell = ncell + n
    ge = lambda a, b: a >= b
    ambiguous = ((count(s1, v1[k - 1], ge) != float(k)) | (count(s2, v2[k - 1], ge) != float(k))
                 | (ncell != float(k)))
    for r1 in range(1, k):
        ambiguous = ambiguous | (v1[r1] + v2[width[r1]] >= tau)
    col = []
    for j in range(1, k + 1):
        c = jnp.full_like(tau, jnp.inf)
        for r1 in range(k // j):
            c = jnp.minimum(c, jnp.where(nrow[r1] >= float(j), v1[r1], jnp.inf))
        col.append(c)
    inv_z = 1.0 / z
    gt = lambda a, b: a > b
    for i in range(N_KEYS):
        store_key(n1_ref, i, count(col, s1[i], lambda c, s: s >= c))
        store_key(r2_ref, i, count(v2, s2[i], gt), 0)
        store_key(e1_ref, i, jnp.where(s1[i] >= v1[k - 1], jnp.exp(s1[i] - v1[0]) * inv_z, 0.0))
        store_key(e2_ref, i, jnp.where(s2[i] >= v2[k - 1], jnp.exp(s2[i] - v2[0]), 0.0), 1)

    @pl.when(jnp.max(ambiguous.astype(F32)) > 0.0)
    def _():
        t1 = jnp.stack(s1, axis=0)
        t2 = jnp.stack(s2, axis=0)
        rank1, w1 = _top16(t1)
        rank2, w2 = _top16(t2)
        crank, _ = _top16(_outer16(w1, w2, jnp.add))
        sel = (crank < float(k)).astype(F32)
        y1 = jnp.exp(w1 - w1[0:1])
        y2 = jnp.exp(w2 - w2[0:1])
        zz = jnp.sum(sel * _outer16(y1, y2, jnp.multiply), axis=0, keepdims=True)
        n1 = jnp.zeros_like(t1)
        for r in range(k):
            cnt = jnp.sum(sel[r * k:(r + 1) * k], axis=0, keepdims=True)
            n1 = jnp.where(rank1 == float(r), cnt, n1)
        g1 = jnp.exp(t1 - w1[0:1]) / zz
        g2 = jnp.exp(t2 - w2[0:1])
        for i in range(N_KEYS):
            store_key(n1_ref, i, n1[i])
            store_key(r2_ref, i, rank2[i], 0)
            store_key(e1_ref, i, g1[i])
            store_key(e2_ref, i, g2[i], 1)


def _gates(scores):
    nblk = scores.shape[2]
    spec = pl.BlockSpec((HEAD_PAIR, 1, SLAB_ROWS, LANE), lambda i, g: (g, i, 0, 0))
    shp = jax.ShapeDtypeStruct((PEER_HEADS, nblk, SLAB_ROWS, LANE), F32)
    pspec = pl.BlockSpec((HEAD_PAIR, 1, SLAB_ROWS + SLAB_PAD, LANE), lambda i, g: (g, i, 0, 0))
    pshp = jax.ShapeDtypeStruct((PEER_HEADS, nblk, SLAB_ROWS + SLAB_PAD, LANE), F32)
    return pl.pallas_call(
        _gates_kernel,
        out_shape=(shp, shp, pshp, pshp),
        grid=(nblk, PEER_HEADS // HEAD_PAIR),
        in_specs=[pl.BlockSpec((HEAD_PAIR, 2, 1, SLAB_ROWS, LANE), lambda i, g: (g, 0, i, 0, 0))],
        out_specs=(spec, spec, pspec, pspec),
        compiler_params=_params("parallel", "parallel"),
        name="gates",
    )(scores)


def _peer_kernel(xnt_ref, x2_ref, u_ref, vt_ref, vtl_ref, n1_ref, e1_ref, r2_ref, e2_ref, gf_ref,
                 o_ref, acc_ref, w_ref, coef_ref, *, ec, t):
    c = pl.program_id(1)

    @pl.when(c == 0)
    def _():
        coef_ref[...] = jnp.zeros((ec, t), BF16)

    kt = MXU_DIM
    for lt in range(t // LANE):
        lanes = slice(lt * LANE, (lt + 1) * LANE)
        for j in range(ec // N_KEYS):
            krow = slice(j * KEY_ROWS + lt, j * KEY_ROWS + lt + 1)
            n1b = [jnp.broadcast_to(n1_ref[h, 0, krow, :], (SUBLANE, LANE)) for h in range(PEER_HEADS)]
            e1b = [jnp.broadcast_to(e1_ref[h, 0, krow, :], (SUBLANE, LANE)) for h in range(PEER_HEADS)]
            for ib in range(N_KEYS // SUBLANE):
                row0 = ib * SUBLANE * KEY_ROWS + lt
                rows = lambda kind, h: pl.ds(row0 + _slab_shift(kind, h), SUBLANE, stride=KEY_ROWS)
                w = jnp.zeros((SUBLANE, LANE), F32)
                for h in range(PEER_HEADS):
                    hit = r2_ref[h, 0, rows(0, h), :] < n1b[h]
                    w = w + jnp.where(hit, e1b[h] * e2_ref[h, 0, rows(1, h), :], 0.0)
                w_ref[j * N_KEYS + ib * SUBLANE:j * N_KEYS + (ib + 1) * SUBLANE, lanes] = w
    prev = jnp.where(c == 0, 0.0, acc_ref[...])
    acc_ref[...] = prev + jnp.dot(vt_ref[:, 0:kt], coef_ref[0:kt, :], preferred_element_type=F32)
    for k0 in range(kt, ec, kt):
        acc_ref[...] += jnp.dot(vt_ref[:, k0:k0 + kt], coef_ref[k0:k0 + kt, :], preferred_element_type=F32)
    act = jax.nn.gelu(jnp.dot(u_ref[...], xnt_ref[...], preferred_element_type=F32))
    coef_ref[...] = (w_ref[...] * act).astype(BF16)

    @pl.when(c == pl.num_programs(1) - 1)
    def _():
        total = acc_ref[...]
        for k0 in range(0, ec, kt):
            total = total + jnp.dot(vtl_ref[:, k0:k0 + kt], coef_ref[k0:k0 + kt, :], preferred_element_type=F32)
        o_ref[...] = _rms(x2_ref[...] + total.T, gf_ref[...])


def _peer(xnt, x2, u_bf16, vt_bf16, n1, e1, r2, e2, gf, t, ec):
    n = xnt.shape[1]
    nc = N_EXPERTS // ec
    assert t == PEER_TOKENS
    kc = ec // N_KEYS
    prv = lambda c: jnp.maximum(c - 1, 0)
    cspec = pl.BlockSpec((PEER_HEADS, 1, kc * KEY_ROWS, LANE), lambda b, c: (0, b, c, 0))
    gspec = pl.BlockSpec((PEER_HEADS, 1, SLAB_ROWS + SLAB_PAD, LANE), lambda b, c: (0, b, 0, 0))
    return pl.pallas_call(
        functools.partial(_peer_kernel, ec=ec, t=t),
        out_shape=jax.ShapeDtypeStruct((n, D_MODEL), F32),
        grid=(n // t, nc),
        in_specs=[
            pl.BlockSpec((D_MODEL, t), lambda b, c: (0, b)),
            pl.BlockSpec((t, D_MODEL), lambda b, c: (b, 0)),
            pl.BlockSpec((ec, D_MODEL), lambda b, c: (c, 0)),
            pl.BlockSpec((D_MODEL, ec), lambda b, c: (0, prv(c))),
            pl.BlockSpec((D_MODEL, ec), lambda b, c: (0, nc - 1)),
            cspec, cspec, gspec, gspec,
            _const_spec((1, D_MODEL)),
        ],
        out_specs=pl.BlockSpec((t, D_MODEL), lambda b, c: (b, 0)),
        scratch_shapes=[
            pltpu.VMEM((D_MODEL, t), F32),
            pltpu.VMEM((ec, t), F32),
            pltpu.VMEM((ec, t), BF16),
        ],
        compiler_params=_params("parallel", "arbitrary"),
        name="peer",
    )(xnt, x2, u_bf16, vt_bf16, vt_bf16, n1, e1, r2, e2, gf)


def _group(x_real, front, hist16, h0r, h0i, wts, *, pos0, t_seq):
    b, l, _ = x_real.shape
    if front is None:
        pad_rows = 0
        proj3 = _inproj(x_real.reshape(1, b * l, D_MODEL), None, wts["norm1_g"], wts["w_in"], T_INPROJ)
        proj3 = proj3.reshape(b, l, W_IN_COLS)
    else:
        pad_rows = front.shape[0]
        proj3 = _inproj(x_real, front, wts["norm1_g"], wts["w_in"], T_INPROJ)
    lp = pad_rows + l
    out_a = _pool(proj3, hist16, wts["pool_w"], wts["pool_scale"], t_seq, pos0)
    out_b, hr, hi = _s5(proj3, h0r, h0i, wts["bblk"], wts["cblk"], wts["scan"], wts["ssm_d"], wts["w_glu"], t_seq)
    n = b * l
    if pad_rows == 0:
        x_real, out_a, out_b, projm = (a.reshape(1, n, a.shape[-1]) for a in (x_real, out_a, out_b, proj3))
    else:
        projm = proj3
    x2, xnt = _merge(x_real, out_a, out_b, projm, wts["proj_a"], wts["proj_b"], wts["w_out"],
                     wts["norm2_g"], T_MERGE, pad_rows // T_MERGE)
    scores = _query(xnt, wts["peer_wqt"], wts["peer_key1"], wts["peer_key2"])
    n1, e1, r2, e2 = _gates(scores)
    y = _peer(xnt, x2.reshape(n, D_MODEL), wts["peer_u"], wts["peer_vt"], n1, e1, r2, e2,
              wts["final_norm_g"], PEER_TOKENS, PEER_CHUNK)
    new_hist = proj3[:, lp - POOL_HIST:, :W_POOL]
    shp = (b, N_SSM_GROUPS, SSM_STATE)
    return y.reshape(b, l, D_MODEL), new_hist, hr.reshape(shp), hi.reshape(shp)


def kernel(x_prompt, x_sample, cache_pool, state_ssm_re, state_ssm_im, meta_tokens, norm1_g, w_in, pool_w, pool_scale, ssm_a_re, ssm_a_im, ssm_log_step, ssm_b_re, ssm_b_im, ssm_c_re, ssm_c_im, ssm_d, w_glu, proj_a, proj_b, w_out, norm2_g, peer_wq, peer_key1, peer_key2, peer_u, peer_v, final_norm_g):
    depth = norm1_g.shape[0]
    assert depth == 1
    bp, seq, _ = x_prompt.shape
    bs, dseq, _ = x_sample.shape

    bblk, cblk, scan = _s5_weights(ssm_a_re[0], ssm_a_im[0], ssm_log_step[0], ssm_b_re[0], ssm_b_im[0],
                                   ssm_c_re[0], ssm_c_im[0])
    wts = {
        "norm1_g": norm1_g[0][None], "w_in": w_in[0].astype(BF16),
        "pool_w": pool_w[0].astype(BF16), "pool_scale": pool_scale[0][None],
        "bblk": bblk, "cblk": cblk, "scan": scan, "ssm_d": ssm_d[0][None], "w_glu": w_glu[0].astype(BF16),
        "proj_a": proj_a[0].astype(BF16), "proj_b": proj_b[0].astype(BF16), "w_out": w_out[0].astype(BF16),
        "norm2_g": norm2_g[0][None], "peer_wqt": peer_wq[0].astype(BF16).T,
        "peer_key1": peer_key1[0].astype(BF16), "peer_key2": peer_key2[0].astype(BF16),
        "peer_u": peer_u[0].astype(BF16), "peer_vt": peer_v[0].astype(BF16).T,
        "final_norm_g": final_norm_g[None],
    }

    front = jnp.concatenate([jnp.zeros((T_INPROJ - N_META, D_MODEL), x_prompt.dtype), meta_tokens], axis=0)
    zst = jnp.zeros((bp, SSM_SB, SSM_SB_LANES), F32)
    yp, pool_p, re_p, im_p = _group(
        x_prompt, front, jnp.zeros((bp, POOL_HIST + 1, W_POOL), F32), zst, zst, wts,
        pos0=1 - (T_INPROJ - N_META), t_seq=T_SEQ)

    hist16 = jnp.pad(cache_pool[0], ((0, 0), (1, 0), (0, 0)))
    ys, pool_s, re_s, im_s = _group(
        x_sample, None, hist16,
        state_ssm_re[0].reshape(bs, SSM_SB, SSM_SB_LANES), state_ssm_im[0].reshape(bs, SSM_SB, SSM_SB_LANES),
        wts, pos0=1 + PAST_LEN, t_seq=dseq)

    return (yp, ys, pool_p[None], re_p[None], im_p[None], pool_s[None], re_s[None], im_s[None])
```

```python
import functools

import jax
import jax.numpy as jnp
from jax import lax
from jax.experimental import pallas as pl
from jax.experimental.pallas import tpu as pltpu

D_MODEL = 2048
N_META = 16
PAST_LEN = 1024
W_POOL = 1024
W_SSM = 1024
POOL_SIZES = (2, 4, 8, 16)
POOL_GROUP = 256
POOL_HIST = 15
N_SSM_GROUPS = 64
SSM_GROUP = 16
SSM_STATE = 64
PEER_HEADS = 8
PEER_TOPK = 16
N_KEYS = 128
N_EXPERTS = N_KEYS * N_KEYS
D_QUERY = 256
D_HALF = 128
W_IN_COLS = W_POOL + W_SSM + 2 * D_MODEL
NORM_EPS = 1e-6

F32 = jnp.float32
BF16 = jnp.bfloat16

SSM_SB = 4
SSM_SB_CH = W_SSM // SSM_SB
SSM_SB_LANES = (N_SSM_GROUPS // SSM_SB) * SSM_STATE
SCAN_ROWS = 8
SCAN_LANES = 256
SCAN_UNROLL = 8
CARRY_LANES = 512

V7X_VMEM_BYTES = 64 * 1024 * 1024
VMEM_LIMIT = V7X_VMEM_BYTES - 8 * 1024 * 1024
MXU_DIM = 256

T_INPROJ = 512
T_SEQ = 256
T_MERGE = 256
INPROJ_COLS = 2048
PEER_CHUNK = 512


def _const_spec(shape):
    nd = len(shape)
    return pl.BlockSpec(shape, lambda *_: (0,) * nd, pipeline_mode=pl.Buffered(1))


def _params(*sem):
    return pltpu.CompilerParams(dimension_semantics=sem, vmem_limit_bytes=VMEM_LIMIT)


def _rms(x, g):
    return x * lax.rsqrt(jnp.mean(x * x, axis=-1, keepdims=True) + NORM_EPS) * g


def _inproj_kernel(x_ref, f_ref, g_ref, w_ref, o_ref, xn_ref, *, front_blocks):
    i = pl.program_id(1)
    first_col = pl.program_id(2) == 0

    @pl.when(first_col & (i >= front_blocks))
    def _():
        xn_ref[...] = _rms(x_ref[0], g_ref[...]).astype(BF16)

    if front_blocks:
        @pl.when(first_col & (i < front_blocks))
        def _():
            xn_ref[...] = _rms(f_ref[...], g_ref[...]).astype(BF16)

    o_ref[0] = jnp.dot(xn_ref[...], w_ref[...], preferred_element_type=F32)


def _inproj(x3, front, g, w_bf16, tm):
    b, l, _ = x3.shape
    cb = INPROJ_COLS
    fb = 0 if front is None else 1
    if front is None:
        front = jnp.zeros((SUBLANE, D_MODEL), F32)
    return pl.pallas_call(
        functools.partial(_inproj_kernel, front_blocks=fb),
        out_shape=jax.ShapeDtypeStruct((b, fb * tm + l, W_IN_COLS), F32),
        grid=(b, fb + l // tm, W_IN_COLS // cb),
        in_specs=[
            pl.BlockSpec((1, tm, D_MODEL), lambda s, i, j: (s, jnp.maximum(i - fb, 0), 0)),
            _const_spec(front.shape),
            _const_spec((1, D_MODEL)),
            pl.BlockSpec((D_MODEL, cb), lambda s, i, j: (0, j)),
        ],
        out_specs=pl.BlockSpec((1, tm, cb), lambda s, i, j: (s, i, j)),
        scratch_shapes=[pltpu.VMEM((tm, D_MODEL), BF16)],
        compiler_params=_params("parallel", "arbitrary", "arbitrary"),
        name="inproj",
    )(x3, front, g, w_bf16)


def _pool_kernel(u_ref, hist_ref, pw_ref, ps_ref, o_ref, ext_ref, *, tp, pos0):
    i = pl.program_id(1)
    hrows = POOL_HIST + 1

    @pl.when(i == 0)
    def _():
        ext_ref[0:hrows, :] = hist_ref[0]

    @pl.when(i > 0)
    def _():
        ext_ref[0:hrows, :] = ext_ref[tp:tp + hrows, :]

    ext_ref[hrows:hrows + tp, :] = u_ref[0]
    row = lax.broadcasted_iota(jnp.int32, (tp, 1), 0)
    avail = i * tp + row + pos0
    for gi, w in enumerate(POOL_SIZES):
        c0 = gi * POOL_GROUP
        u = ext_ref[hrows:hrows + tp, c0:c0 + POOL_GROUP]
        acc = u
        for j in range(1, w):
            acc = acc + ext_ref[hrows - j:hrows - j + tp, c0:c0 + POOL_GROUP]
        cnt = jnp.maximum(jnp.minimum(avail, w), 1).astype(F32)
        pooled = acc / cnt - u
        mixed = jnp.dot(pooled.astype(BF16), pw_ref[gi], preferred_element_type=F32)
        o_ref[0, :, c0:c0 + POOL_GROUP] = (mixed * ps_ref[:, c0:c0 + POOL_GROUP]).astype(o_ref.dtype)


def _pool(proj3, hist16, pw_bf16, pscale, tp, pos0):
    b, l, _ = proj3.shape
    return pl.pallas_call(
        functools.partial(_pool_kernel, tp=tp, pos0=pos0),
        out_shape=jax.ShapeDtypeStruct((b, l, W_POOL), BF16),
        grid=(b, l // tp),
        in_specs=[
            pl.BlockSpec((1, tp, W_POOL), lambda s, i: (s, i, 0)),
            pl.BlockSpec((1, POOL_HIST + 1, W_POOL), lambda s, i: (s, 0, 0)),
            _const_spec((len(POOL_SIZES), POOL_GROUP, POOL_GROUP)),
            _const_spec((1, W_POOL)),
        ],
        out_specs=pl.BlockSpec((1, tp, W_POOL), lambda s, i: (s, i, 0)),
        scratch_shapes=[pltpu.VMEM((POOL_HIST + 1 + tp, W_POOL), F32)],
        compiler_params=_params("parallel", "arbitrary"),
        name="pool",
    )(proj3, hist16, pw_bf16, pscale)


def _s5_kernel(u_ref, h0r_ref, h0i_ref, bblk_ref, cblk_ref, sc_ref, d_ref, wglu_ref,
               o_ref, hr_out, hi_out, bu_ref, xs_ref, y_ref, cr_ref, ci_ref, *, ts):
    i = pl.program_id(1)

    @pl.when(i == 0)
    def _():
        cr_ref[...] = h0r_ref[0]
        ci_ref[...] = h0i_ref[0]

    u = u_ref[0]
    ub = u.astype(BF16)
    for sb in range(SSM_SB):
        bu_ref[...] = jnp.dot(ub[:, sb * SSM_SB_CH:(sb + 1) * SSM_SB_CH], bblk_ref[sb],
                              preferred_element_type=F32)
        for lc in range(SSM_SB_LANES // SCAN_LANES):
            re = slice(lc * SCAN_LANES, (lc + 1) * SCAN_LANES)
            im = slice(SSM_SB_LANES + lc * SCAN_LANES, SSM_SB_LANES + (lc + 1) * SCAN_LANES)
            a1r, a1i, a2r, a2i, a4r, a4i = [sc_ref[sb, k, :, re] for k in range(6)]
            steps = ((a1r, a1i, 1), (a2r, a2i, 2), (a4r, a4i, 4))

            def local_scan(r, _, re=re, im=im, steps=steps):
                rows = pl.ds(pl.multiple_of(r * SCAN_ROWS, SCAN_ROWS), SCAN_ROWS)
                xr = bu_ref[rows, re]
                xi = bu_ref[rows, im]
                for ar, ai, d in steps:
                    sr = pltpu.roll(xr, d, 0)
                    si = pltpu.roll(xi, d, 0)
                    xr, xi = xr + ar * sr - ai * si, xi + ar * si + ai * sr
                xs_ref[rows, re] = xr
                xs_ref[rows, im] = xi
                return 0

            lax.fori_loop(0, ts // SCAN_ROWS, local_scan, 0, unroll=SCAN_UNROLL)
        for lc in range(SSM_SB_LANES // CARRY_LANES):
            re = slice(lc * CARRY_LANES, (lc + 1) * CARRY_LANES)
            im = slice(SSM_SB_LANES + lc * CARRY_LANES, SSM_SB_LANES + (lc + 1) * CARRY_LANES)
            pr = sc_ref[sb, 6, :, re]
            pi = sc_ref[sb, 7, :, re]
            cr0 = jnp.broadcast_to(cr_ref[sb:sb + 1, re], (SCAN_ROWS, CARRY_LANES))
            ci0 = jnp.broadcast_to(ci_ref[sb:sb + 1, re], (SCAN_ROWS, CARRY_LANES))

            def add_carry(r, carry, re=re, im=im, pr=pr, pi=pi):
                cr, ci = carry
                rows = pl.ds(pl.multiple_of(r * SCAN_ROWS, SCAN_ROWS), SCAN_ROWS)
                hr = xs_ref[rows, re] + pr * cr - pi * ci
                hi = xs_ref[rows, im] + pr * ci + pi * cr
                bu_ref[rows, re] = hr
                bu_ref[rows, im] = hi
                last = slice(SCAN_ROWS - 1, SCAN_ROWS)
                return (jnp.broadcast_to(hr[last], (SCAN_ROWS, CARRY_LANES)),
                        jnp.broadcast_to(hi[last], (SCAN_ROWS, CARRY_LANES)))

            cr, ci = lax.fori_loop(0, ts // SCAN_ROWS, add_carry, (cr0, ci0), unroll=4)
            cr_ref[sb:sb + 1, re] = cr[0:1]
            ci_ref[sb:sb + 1, re] = ci[0:1]
        y_ref[:, sb * SSM_SB_CH:(sb + 1) * SSM_SB_CH] = jnp.dot(
            bu_ref[...].astype(BF16), cblk_ref[sb], preferred_element_type=F32)
    y = jax.nn.gelu(y_ref[...] + d_ref[...] * u)
    z = jnp.dot(y.astype(BF16), wglu_ref[...], preferred_element_type=F32)
    o_ref[0] = (y * jax.nn.sigmoid(z)).astype(o_ref.dtype)

    @pl.when(i == pl.num_programs(1) - 1)
    def _():
        hr_out[0] = cr_ref[...]
        hi_out[0] = ci_ref[...]


def _s5(proj3, h0r, h0i, bblk, cblk, scan_consts, d_skip, wglu_bf16, ts):
    b, l, _ = proj3.shape
    st_spec = pl.BlockSpec((1, SSM_SB, SSM_SB_LANES), lambda s, i: (s, 0, 0))
    st_shape = jax.ShapeDtypeStruct((b, SSM_SB, SSM_SB_LANES), F32)
    return pl.pallas_call(
        functools.partial(_s5_kernel, ts=ts),
        out_shape=(jax.ShapeDtypeStruct((b, l, W_SSM), BF16), st_shape, st_shape),
        grid=(b, l // ts),
        in_specs=[
            pl.BlockSpec((1, ts, W_SSM), lambda s, i: (s, i, 1)),
            st_spec, st_spec,
            _const_spec(bblk.shape), _const_spec(cblk.shape), _const_spec(scan_consts.shape),
            _const_spec((1, W_SSM)), _const_spec((W_SSM, W_SSM)),
        ],
        out_specs=(pl.BlockSpec((1, ts, W_SSM), lambda s, i: (s, i, 0)), st_spec, st_spec),
        scratch_shapes=[
            pltpu.VMEM((ts, 2 * SSM_SB_LANES), F32),
            pltpu.VMEM((ts, 2 * SSM_SB_LANES), F32),
            pltpu.VMEM((ts, W_SSM), F32),
            pltpu.VMEM((SSM_SB, SSM_SB_LANES), F32),
            pltpu.VMEM((SSM_SB, SSM_SB_LANES), F32),
        ],
        compiler_params=_params("parallel", "arbitrary"),
        name="s5",
    )(proj3, h0r, h0i, bblk, cblk, scan_consts, d_skip, wglu_bf16)


def _s5_weights(a_re, a_im, log_step, b_re, b_im, c_re, c_im):
    dt = jnp.exp(log_step)[:, None]
    mag = jnp.exp(a_re * dt)
    abr = mag * jnp.cos(a_im * dt)
    abi = mag * jnp.sin(a_im * dt)
    nr, ni = abr - 1.0, abi
    den = a_re * a_re + a_im * a_im
    f_re = (nr * a_re + ni * a_im) / den
    f_im = (ni * a_re - nr * a_im) / den
    bbr = f_re[..., None] * b_re - f_im[..., None] * b_im
    bbi = f_re[..., None] * b_im + f_im[..., None] * b_re
    gl = N_SSM_GROUPS // SSM_SB
    eye = jnp.eye(gl, dtype=F32)

    def bdiag_in(x):
        x = x.reshape(SSM_SB, gl, SSM_STATE, SSM_GROUP)
        return jnp.einsum("sgpc,gh->sgchp", x, eye).reshape(SSM_SB, gl * SSM_GROUP, gl * SSM_STATE)

    def bdiag_out(x):
        x = x.reshape(SSM_SB, gl, SSM_GROUP, SSM_STATE)
        return jnp.einsum("sgcp,gh->sgphc", x, eye).reshape(SSM_SB, gl * SSM_STATE, gl * SSM_GROUP)

    bblk = jnp.concatenate([bdiag_in(bbr), bdiag_in(bbi)], axis=2).astype(BF16)
    cblk = jnp.concatenate([bdiag_out(c_re), -bdiag_out(c_im)], axis=1).astype(BF16)

    ar = abr.reshape(SSM_SB, SSM_SB_LANES)
    ai = abi.reshape(SSM_SB, SSM_SB_LANES)
    pw_r, pw_i = [ar], [ai]
    for _ in range(SCAN_ROWS - 1):
        pr, pi = pw_r[-1], pw_i[-1]
        pw_r.append(pr * ar - pi * ai)
        pw_i.append(pr * ai + pi * ar)
    row = jnp.arange(SCAN_ROWS)[None, :, None]

    def step(d):
        m = (row >= d).astype(F32)
        return pw_r[d - 1][:, None, :] * m, pw_i[d - 1][:, None, :] * m

    s1r, s1i = step(1)
    s2r, s2i = step(2)
    s4r, s4i = step(4)
    car_r = jnp.stack(pw_r, axis=1)
    car_i = jnp.stack(pw_i, axis=1)
    consts = jnp.stack([s1r, s1i, s2r, s2i, s4r, s4i, car_r, car_i], axis=1)
    return bblk, cblk, consts


def _merge_kernel(x_ref, oa_ref, ob_ref, ga_ref, gb_ref, pa_ref, pb_ref, wo_ref, g2_ref,
                  x2_ref, xnt_ref):
    ma = jnp.dot(oa_ref[0], pa_ref[...], preferred_element_type=F32)
    mb = jnp.dot(ob_ref[0], pb_ref[...], preferred_element_type=F32)
    merged = jax.nn.sigmoid(ga_ref[0]) * ma + jax.nn.sigmoid(gb_ref[0]) * mb
    x2 = x_ref[0] + jnp.dot(merged.astype(BF16), wo_ref[...], preferred_element_type=F32)
    x2_ref[0] = x2
    xnt_ref[...] = _rms(x2, g2_ref[...]).T.astype(BF16)


def _merge(x3, oa3, ob3, proj3, pa, pb, wo, g2, tm, pad_blocks):
    b, l, _ = x3.shape
    off = pad_blocks
    nblk = l // tm
    return pl.pallas_call(
        _merge_kernel,
        out_shape=(jax.ShapeDtypeStruct((b, l, D_MODEL), F32), jax.ShapeDtypeStruct((D_MODEL, b * l), BF16)),
        grid=(b, nblk),
        in_specs=[
            pl.BlockSpec((1, tm, D_MODEL), lambda s, i: (s, i, 0)),
            pl.BlockSpec((1, tm, W_POOL), lambda s, i: (s, i + off, 0)),
            pl.BlockSpec((1, tm, W_SSM), lambda s, i: (s, i + off, 0)),
            pl.BlockSpec((1, tm, D_MODEL), lambda s, i: (s, i + off, 1)),
            pl.BlockSpec((1, tm, D_MODEL), lambda s, i: (s, i + off, 2)),
            _const_spec((W_POOL, D_MODEL)), _const_spec((W_SSM, D_MODEL)),
            _const_spec((D_MODEL, D_MODEL)), _const_spec((1, D_MODEL)),
        ],
        out_specs=(pl.BlockSpec((1, tm, D_MODEL), lambda s, i: (s, i, 0)),
                   pl.BlockSpec((D_MODEL, tm), lambda s, i: (0, s * nblk + i))),
        compiler_params=_params("parallel", "parallel"),
        name="merge",
    )(x3, oa3, ob3, proj3, proj3, pa, pb, wo, g2)


LANE = 128
SUBLANE = 8
PACK_ROWS = 16
PEER_TOKENS = 512
KEY_ROWS = PEER_TOKENS // LANE
SLAB_ROWS = N_KEYS * KEY_ROWS
HEAD_PAIR = SUBLANE // KEY_ROWS
SLAB_PAD = SUBLANE


def _slab_shift(kind, head):
    return kind + 2 * (head % HEAD_PAIR)


def _query_kernel(xnt_ref, wqt_ref, k1_ref, k2_ref, s_ref):
    qt = jnp.dot(wqt_ref[...], xnt_ref[...], preferred_element_type=F32).astype(BF16)
    for h in range(PEER_HEADS):
        r0 = h * D_QUERY
        halves = (jnp.dot(k1_ref[...], qt[r0:r0 + D_HALF], preferred_element_type=F32),
                  jnp.dot(k2_ref[...], qt[r0 + D_HALF:r0 + D_QUERY], preferred_element_type=F32))
        for half, s in enumerate(halves):
            for j in range(KEY_ROWS):
                s_ref[h, half, 0, pl.ds(j, N_KEYS, stride=KEY_ROWS), :] = s[:, j * 128:(j + 1) * 128]


def _query(xnt, wqt, k1, k2):
    n = xnt.shape[1]
    nblk = n // PEER_TOKENS
    return pl.pallas_call(
        _query_kernel,
        out_shape=jax.ShapeDtypeStruct((PEER_HEADS, 2, nblk, SLAB_ROWS, 128), F32),
        grid=(nblk,),
        in_specs=[
            pl.BlockSpec((D_MODEL, PEER_TOKENS), lambda i: (0, i)),
            _const_spec((PEER_HEADS * D_QUERY, D_MODEL)),
            _const_spec((N_KEYS, D_HALF)), _const_spec((N_KEYS, D_HALF)),
        ],
        out_specs=pl.BlockSpec((PEER_HEADS, 2, 1, SLAB_ROWS, 128), lambda i: (0, 0, i, 0, 0)),
        compiler_params=_params("parallel"),
        name="query",
    )(xnt, wqt, k1, k2)


def _oddeven_merge_sort_pairs(n):
    pairs = []
    p = 1
    while p < n:
        k = p
        while k >= 1:
            for j in range(k % p, n - k, 2 * k):
                for i in range(min(k, n - j - k)):
                    if (i + j) // (p * 2) == (i + j + k) // (p * 2):
                        pairs.append((i + j, i + j + k))
            k //= 2
        p *= 2
    return pairs


_SORT16_PAIRS = _oddeven_merge_sort_pairs(PEER_TOPK)


def _cmpx(v, i, j):
    a, b = v[i], v[j]
    v[i] = jnp.maximum(a, b)
    v[j] = jnp.minimum(a, b)


def _sort16(v):
    v = list(v)
    for i, j in _SORT16_PAIRS:
        _cmpx(v, i, j)
    return v


def _merge_top16(a, b):
    k = PEER_TOPK
    c = [a[i] if b[k - 1 - i] is None else jnp.maximum(a[i], b[k - 1 - i]) for i in range(k)]
    d = k // 2
    while d >= 1:
        for i in range(k):
            if not i & d:
                _cmpx(c, i, i + d)
        d //= 2
    return c


def _top16_values(rows):
    groups = [_sort16(rows[g:g + PEER_TOPK]) for g in range(0, len(rows), PEER_TOPK)]
    while len(groups) > 1:
        groups = [_merge_top16(groups[g], groups[g + 1]) for g in range(0, len(groups), 2)]
    return groups[0]


def _top16(val):
    rows = val.shape[0]
    rowid = lax.broadcasted_iota(jnp.int32, val.shape, 0).astype(F32)
    kshape = (PEER_TOPK,) + val.shape[1:]
    kid = lax.broadcasted_iota(jnp.int32, kshape, 0)

    def body(r, carry):
        val, rank, tops = carry
        m = jnp.max(val, axis=0, keepdims=True)
        idx = jnp.min(jnp.where(val == m, rowid, float(rows)), axis=0, keepdims=True)
        hit = rowid == idx
        rank = jnp.where(hit, lax.convert_element_type(r, F32), rank)
        val = jnp.where(hit, -jnp.inf, val)
        tops = jnp.where(kid == r, m, tops)
        return val, rank, tops

    init = (val, jnp.full(val.shape, float(PEER_TOPK), F32), jnp.zeros(kshape, F32))
    _, rank, tops = lax.fori_loop(0, PEER_TOPK, body, init)
    return rank, tops


def _outer16(a, b, op):
    return jnp.concatenate([op(a[r:r + 1], b) for r in range(PEER_TOPK)], axis=0)


def _gates_kernel(s_ref, c1_ref, e1_ref, a2_ref, e2_ref):
    k = PEER_TOPK

    def load_key(half, i):
        rows = slice(i * KEY_ROWS, (i + 1) * KEY_ROWS)
        return jnp.concatenate([s_ref[hh, half, 0, rows, :] for hh in range(HEAD_PAIR)], axis=0)

    def store_key(ref, i, val, kind=None):
        for hh in range(HEAD_PAIR):
            shift = 0 if kind is None else _slab_shift(kind, hh)
            rows = slice(i * KEY_ROWS + shift, (i + 1) * KEY_ROWS + shift)
            ref[hh, 0, rows, :] = val[hh * KEY_ROWS:(hh + 1) * KEY_ROWS]

    for ref in (a2_ref, e2_ref):
        for hh in range(HEAD_PAIR):
            ref[hh, 0, 0:SUBLANE, :] = jnp.zeros((SUBLANE, LANE), F32)
            ref[hh, 0, SLAB_ROWS:SLAB_ROWS + SLAB_PAD, :] = jnp.zeros((SLAB_PAD, LANE), F32)
    s1 = [load_key(0, i) for i in range(N_KEYS)]
    s2 = [load_key(1, i) for i in range(N_KEYS)]
    v1 = _top16_values(s1)
    v2 = _top16_values(s2)
    width = [k // (r + 1) for r in range(k)]
    cell = [[v1[r1] + v2[r2] for r2 in range(width[r1])] for r1 in range(k)]
    top = cell[0]
    for r1 in range(1, k):
        top = _merge_top16(top, cell[r1] + [None] * (k - width[r1]))
    tau = top[k - 1]

    def count(vals, thr, cmp):
        n = jnp.zeros_like(thr)
        for v in vals:
            n = n + cmp(v, thr).astype(F32)
        return n

    x1 = [jnp.exp(v - v1[0]) for v in v1]
    x2 = [jnp.exp(v - v2[0]) for v in v2]
    z = jnp.zeros_like(tau)
    nrow = []
    for r1 in range(k):
        n = jnp.zeros_like(tau)
        for r2 in range(width[r1]):
            hit = cell[r1][r2] >= tau
            n = n + hit.astype(F32)
            z = z + jnp.where(hit, x1[r1] * x2[r2], 0.0)
        nrow.append(n)
    ncell = nrow[0]
    for n in nrow[1:]:
        ncell = ncell + n
    ge = lambda a, b: a >= b
    ambiguous = ((count(s1, v1[k - 1], ge) != float(k)) | (count(s2, v2[k - 1], ge) != float(k))
                 | (ncell != float(k)))
    for r1 in range(1, k):
        ambiguous = ambiguous | (v1[r1] + v2[width[r1]] >= tau)
    col = []
    for j in range(1, k + 1):
        c = jnp.full_like(tau, jnp.inf)
        for r1 in range(k // j):
            c = jnp.minimum(c, jnp.where(nrow[r1] >= float(j), v1[r1], jnp.inf))
        col.append(c)
    inv_z = 1.0 / z
    for i in range(N_KEYS):
        c1 = jnp.full_like(tau, jnp.inf)
        for j in range(k):
            c1 = jnp.minimum(c1, jnp.where(s1[i] >= col[j], v2[j], jnp.inf))
        store_key(c1_ref, i, c1)
        store_key(a2_ref, i, s2[i], 0)
        store_key(e1_ref, i, jnp.where(s1[i] >= v1[k - 1], jnp.exp(s1[i] - v1[0]) * inv_z, 0.0))
        store_key(e2_ref, i, jnp.where(s2[i] >= v2[k - 1], jnp.exp(s2[i] - v2[0]), 0.0), 1)

    @pl.when(jnp.max(ambiguous.astype(F32)) > 0.0)
    def _():
        t1 = jnp.stack(s1, axis=0)
        t2 = jnp.stack(s2, axis=0)
        rank1, w1 = _top16(t1)
        rank2, w2 = _top16(t2)
        crank, _ = _top16(_outer16(w1, w2, jnp.add))
        sel = (crank < float(k)).astype(F32)
        y1 = jnp.exp(w1 - w1[0:1])
        y2 = jnp.exp(w2 - w2[0:1])
        zz = jnp.sum(sel * _outer16(y1, y2, jnp.multiply), axis=0, keepdims=True)
        n1 = jnp.zeros_like(t1)
        for r in range(k):
            cnt = jnp.sum(sel[r * k:(r + 1) * k], axis=0, keepdims=True)
            n1 = jnp.where(rank1 == float(r), cnt, n1)
        g1 = jnp.exp(t1 - w1[0:1]) / zz
        g2 = jnp.exp(t2 - w2[0:1])
        for i in range(N_KEYS):
            store_key(c1_ref, i, 0.5 - n1[i])
            store_key(a2_ref, i, -rank2[i], 0)
            store_key(e1_ref, i, g1[i])
            store_key(e2_ref, i, g2[i], 1)


def _gates(scores):
    nblk = scores.shape[2]
    spec = pl.BlockSpec((HEAD_PAIR, 1, SLAB_ROWS, LANE), lambda i, g: (g, i, 0, 0))
    shp = jax.ShapeDtypeStruct((PEER_HEADS, nblk, SLAB_ROWS, LANE), F32)
    pspec = pl.BlockSpec((HEAD_PAIR, 1, SLAB_ROWS + SLAB_PAD, LANE), lambda i, g: (g, i, 0, 0))
    pshp = jax.ShapeDtypeStruct((PEER_HEADS, nblk, SLAB_ROWS + SLAB_PAD, LANE), F32)
    return pl.pallas_call(
        _gates_kernel,
        out_shape=(shp, shp, pshp, pshp),
        grid=(nblk, PEER_HEADS // HEAD_PAIR),
        in_specs=[pl.BlockSpec((HEAD_PAIR, 2, 1, SLAB_ROWS, LANE), lambda i, g: (g, 0, i, 0, 0))],
        out_specs=(spec, spec, pspec, pspec),
        compiler_params=_params("parallel", "parallel"),
        name="gates",
    )(scores)


def _peer_kernel(xnt_ref, x2_ref, u_ref, vt_ref, vtl_ref, c1_ref, e1_ref, a2_ref, e2_ref, gf_ref,
                 o_ref, acc_ref, w_ref, coef_ref, *, ec, t):
    c = pl.program_id(1)

    @pl.when(c == 0)
    def _():
        coef_ref[...] = jnp.zeros((ec, t), BF16)

    kt = MXU_DIM
    for lt in range(t // LANE):
        lanes = slice(lt * LANE, (lt + 1) * LANE)
        for j in range(ec // N_KEYS):
            krow = slice(j * KEY_ROWS + lt, j * KEY_ROWS + lt + 1)
            c1b = [jnp.broadcast_to(c1_ref[h, 0, krow, :], (SUBLANE, LANE)) for h in range(PEER_HEADS)]
            e1b = [jnp.broadcast_to(e1_ref[h, 0, krow, :], (SUBLANE, LANE)) for h in range(PEER_HEADS)]
            for ib in range(N_KEYS // SUBLANE):
                row0 = ib * SUBLANE * KEY_ROWS + lt
                rows = lambda kind, h: pl.ds(row0 + _slab_shift(kind, h), SUBLANE, stride=KEY_ROWS)
                w = jnp.zeros((SUBLANE, LANE), F32)
                for h in range(PEER_HEADS):
                    hit = a2_ref[h, 0, rows(0, h), :] >= c1b[h]
                    w = w + jnp.where(hit, e1b[h] * e2_ref[h, 0, rows(1, h), :], 0.0)
                w_ref[j * N_KEYS + ib * SUBLANE:j * N_KEYS + (ib + 1) * SUBLANE, lanes] = w
    prev = jnp.where(c == 0, 0.0, acc_ref[...])
    acc_ref[...] = prev + jnp.dot(vt_ref[:, 0:kt], coef_ref[0:kt, :], preferred_element_type=F32)
    for k0 in range(kt, ec, kt):
        acc_ref[...] += jnp.dot(vt_ref[:, k0:k0 + kt], coef_ref[k0:k0 + kt, :], preferred_element_type=F32)
    act = jax.nn.gelu(jnp.dot(u_ref[...], xnt_ref[...], preferred_element_type=F32))
    coef_ref[...] = (w_ref[...] * act).astype(BF16)

    @pl.when(c == pl.num_programs(1) - 1)
    def _():
        total = acc_ref[...]
        for k0 in range(0, ec, kt):
            total = total + jnp.dot(vtl_ref[:, k0:k0 + kt], coef_ref[k0:k0 + kt, :], preferred_element_type=F32)
        o_ref[...] = _rms(x2_ref[...] + total.T, gf_ref[...])


def _peer(xnt, x2, u_bf16, vt_bf16, n1, e1, r2, e2, gf, t, ec):
    n = xnt.shape[1]
    nc = N_EXPERTS // ec
    assert t == PEER_TOKENS
    kc = ec // N_KEYS
    prv = lambda c: jnp.maximum(c - 1, 0)
    cspec = pl.BlockSpec((PEER_HEADS, 1, kc * KEY_ROWS, LANE), lambda b, c: (0, b, c, 0))
    gspec = pl.BlockSpec((PEER_HEADS, 1, SLAB_ROWS + SLAB_PAD, LANE), lambda b, c: (0, b, 0, 0))
    return pl.pallas_call(
        functools.partial(_peer_kernel, ec=ec, t=t),
        out_shape=jax.ShapeDtypeStruct((n, D_MODEL), F32),
        grid=(n // t, nc),
        in_specs=[
            pl.BlockSpec((D_MODEL, t), lambda b, c: (0, b)),
            pl.BlockSpec((t, D_MODEL), lambda b, c: (b, 0)),
            pl.BlockSpec((ec, D_MODEL), lambda b, c: (c, 0)),
            pl.BlockSpec((D_MODEL, ec), lambda b, c: (0, prv(c))),
            pl.BlockSpec((D_MODEL, ec), lambda b, c: (0, nc - 1)),
            cspec, cspec, gspec, gspec,
            _const_spec((1, D_MODEL)),
        ],
        out_specs=pl.BlockSpec((t, D_MODEL), lambda b, c: (b, 0)),
        scratch_shapes=[
            pltpu.VMEM((D_MODEL, t), F32),
            pltpu.VMEM((ec, t), F32),
            pltpu.VMEM((ec, t), BF16),
        ],
        compiler_params=_params("parallel", "arbitrary"),
        name="peer",
    )(xnt, x2, u_bf16, vt_bf16, vt_bf16, n1, e1, r2, e2, gf)


def _group(x_real, front, hist16, h0r, h0i, wts, *, pos0, t_seq):
    b, l, _ = x_real.shape
    if front is None:
        pad_rows = 0
        proj3 = _inproj(x_real.reshape(1, b * l, D_MODEL), None, wts["norm1_g"], wts["w_in"], T_INPROJ)
        proj3 = proj3.reshape(b, l, W_IN_COLS)
    else:
        pad_rows = front.shape[0]
        proj3 = _inproj(x_real, front, wts["norm1_g"], wts["w_in"], T_INPROJ)
    lp = pad_rows + l
    out_a = _pool(proj3, hist16, wts["pool_w"], wts["pool_scale"], t_seq, pos0)
    out_b, hr, hi = _s5(proj3, h0r, h0i, wts["bblk"], wts["cblk"], wts["scan"], wts["ssm_d"], wts["w_glu"], t_seq)
    n = b * l
    if pad_rows == 0:
        x_real, out_a, out_b, projm = (a.reshape(1, n, a.shape[-1]) for a in (x_real, out_a, out_b, proj3))
    else:
        projm = proj3
    x2, xnt = _merge(x_real, out_a, out_b, projm, wts["proj_a"], wts["proj_b"], wts["w_out"],
                     wts["norm2_g"], T_MERGE, pad_rows // T_MERGE)
    scores = _query(xnt, wts["peer_wqt"], wts["peer_key1"], wts["peer_key2"])
    n1, e1, r2, e2 = _gates(scores)
    y = _peer(xnt, x2.reshape(n, D_MODEL), wts["peer_u"], wts["peer_vt"], n1, e1, r2, e2,
              wts["final_norm_g"], PEER_TOKENS, PEER_CHUNK)
    new_hist = proj3[:, lp - POOL_HIST:, :W_POOL]
    shp = (b, N_SSM_GROUPS, SSM_STATE)
    return y.reshape(b, l, D_MODEL), new_hist, hr.reshape(shp), hi.reshape(shp)


def kernel(x_prompt, x_sample, cache_pool, state_ssm_re, state_ssm_im, meta_tokens, norm1_g, w_in, pool_w, pool_scale, ssm_a_re, ssm_a_im, ssm_log_step, ssm_b_re, ssm_b_im, ssm_c_re, ssm_c_im, ssm_d, w_glu, proj_a, proj_b, w_out, norm2_g, peer_wq, peer_key1, peer_key2, peer_u, peer_v, final_norm_g):
    depth = norm1_g.shape[0]
    assert depth == 1
    bp, seq, _ = x_prompt.shape
    bs, dseq, _ = x_sample.shape

    bblk, cblk, scan = _s5_weights(ssm_a_re[0], ssm_a_im[0], ssm_log_step[0], ssm_b_re[0], ssm_b_im[0],
                                   ssm_c_re[0], ssm_c_im[0])
    wts = {
        "norm1_g": norm1_g[0][None], "w_in": w_in[0].astype(BF16),
        "pool_w": pool_w[0].astype(BF16), "pool_scale": pool_scale[0][None],
        "bblk": bblk, "cblk": cblk, "scan": scan, "ssm_d": ssm_d[0][None], "w_glu": w_glu[0].astype(BF16),
        "proj_a": proj_a[0].astype(BF16), "proj_b": proj_b[0].astype(BF16), "w_out": w_out[0].astype(BF16),
        "norm2_g": norm2_g[0][None], "peer_wqt": peer_wq[0].astype(BF16).T,
        "peer_key1": peer_key1[0].astype(BF16), "peer_key2": peer_key2[0].astype(BF16),
        "peer_u": peer_u[0].astype(BF16), "peer_vt": peer_v[0].astype(BF16).T,
        "final_norm_g": final_norm_g[None],
    }

    front = jnp.concatenate([jnp.zeros((T_INPROJ - N_META, D_MODEL), x_prompt.dtype), meta_tokens], axis=0)
    zst = jnp.zeros((bp, SSM_SB, SSM_SB_LANES), F32)
    yp, pool_p, re_p, im_p = _group(
        x_prompt, front, jnp.zeros((bp, POOL_HIST + 1, W_POOL), F32), zst, zst, wts,
        pos0=1 - (T_INPROJ - N_META), t_seq=T_SEQ)

    hist16 = jnp.pad(cache_pool[0], ((0, 0), (1, 0), (0, 0)))
    ys, pool_s, re_s, im_s = _group(
        x_sample, None, hist16,
        state_ssm_re[0].reshape(bs, SSM_SB, SSM_SB_LANES), state_ssm_im[0].reshape(bs, SSM_SB, SSM_SB_LANES),
        wts, pos0=1 + PAST_LEN, t_seq=dseq)

    return (yp, ys, pool_p[None], re_p[None], im_p[None], pool_s[None], re_s[None], im_s[None])
```

```python
import functools

import jax
import jax.numpy as jnp
from jax import lax
from jax.experimental import pallas as pl
from jax.experimental.pallas import tpu as pltpu

D_MODEL = 2048
N_META = 16
PAST_LEN = 1024
W_POOL = 1024
W_SSM = 1024
POOL_SIZES = (2, 4, 8, 16)
POOL_GROUP = 256
POOL_HIST = 15
N_SSM_GROUPS = 64
SSM_GROUP = 16
SSM_STATE = 64
PEER_HEADS = 8
PEER_TOPK = 16
N_KEYS = 128
N_EXPERTS = N_KEYS * N_KEYS
D_QUERY = 256
D_HALF = 128
W_IN_COLS = W_POOL + W_SSM + 2 * D_MODEL
NORM_EPS = 1e-6

F32 = jnp.float32
BF16 = jnp.bfloat16

SSM_SB = 4
SSM_SB_CH = W_SSM // SSM_SB
SSM_SB_LANES = (N_SSM_GROUPS // SSM_SB) * SSM_STATE
SCAN_ROWS = 8
SCAN_LANES = 256
SCAN_UNROLL = 8
CARRY_LANES = 512

V7X_VMEM_BYTES = 64 * 1024 * 1024
VMEM_LIMIT = V7X_VMEM_BYTES - 8 * 1024 * 1024
MXU_DIM = 256

T_INPROJ = 512
T_SEQ = 512
T_MERGE = 256
INPROJ_COLS = 2048
PEER_CHUNK = 512


def _const_spec(shape):
    nd = len(shape)
    return pl.BlockSpec(shape, lambda *_: (0,) * nd, pipeline_mode=pl.Buffered(1))


def _params(*sem):
    return pltpu.CompilerParams(dimension_semantics=sem, vmem_limit_bytes=VMEM_LIMIT)


def _rms(x, g):
    return x * lax.rsqrt(jnp.mean(x * x, axis=-1, keepdims=True) + NORM_EPS) * g


def _inproj_kernel(x_ref, f_ref, g_ref, w_ref, o_ref, xn_ref, *, front_blocks):
    i = pl.program_id(1)
    first_col = pl.program_id(2) == 0

    @pl.when(first_col & (i >= front_blocks))
    def _():
        xn_ref[...] = _rms(x_ref[0], g_ref[...]).astype(BF16)

    if front_blocks:
        @pl.when(first_col & (i < front_blocks))
        def _():
            xn_ref[...] = _rms(f_ref[...], g_ref[...]).astype(BF16)

    o_ref[0] = jnp.dot(xn_ref[...], w_ref[...], preferred_element_type=F32)


def _inproj(x3, front, g, w_bf16, tm):
    b, l, _ = x3.shape
    cb = INPROJ_COLS
    fb = 0 if front is None else 1
    if front is None:
        front = jnp.zeros((SUBLANE, D_MODEL), F32)
    return pl.pallas_call(
        functools.partial(_inproj_kernel, front_blocks=fb),
        out_shape=jax.ShapeDtypeStruct((b, fb * tm + l, W_IN_COLS), F32),
        grid=(b, fb + l // tm, W_IN_COLS // cb),
        in_specs=[
            pl.BlockSpec((1, tm, D_MODEL), lambda s, i, j: (s, jnp.maximum(i - fb, 0), 0)),
            _const_spec(front.shape),
            _const_spec((1, D_MODEL)),
            pl.BlockSpec((D_MODEL, cb), lambda s, i, j: (0, j)),
        ],
        out_specs=pl.BlockSpec((1, tm, cb), lambda s, i, j: (s, i, j)),
        scratch_shapes=[pltpu.VMEM((tm, D_MODEL), BF16)],
        compiler_params=_params("parallel", "arbitrary", "arbitrary"),
        name="inproj",
    )(x3, front, g, w_bf16)


def _pool_kernel(u_ref, hist_ref, pw_ref, ps_ref, o_ref, ext_ref, *, tp, pos0):
    i = pl.program_id(1)
    hrows = POOL_HIST + 1

    @pl.when(i == 0)
    def _():
        ext_ref[0:hrows, :] = hist_ref[0]

    @pl.when(i > 0)
    def _():
        ext_ref[0:hrows, :] = ext_ref[tp:tp + hrows, :]

    ext_ref[hrows:hrows + tp, :] = u_ref[0]
    row = lax.broadcasted_iota(jnp.int32, (tp, 1), 0)
    avail = i * tp + row + pos0
    for gi, w in enumerate(POOL_SIZES):
        c0 = gi * POOL_GROUP
        u = ext_ref[hrows:hrows + tp, c0:c0 + POOL_GROUP]
        acc = u
        for j in range(1, w):
            acc = acc + ext_ref[hrows - j:hrows - j + tp, c0:c0 + POOL_GROUP]
        cnt = jnp.maximum(jnp.minimum(avail, w), 1).astype(F32)
        pooled = acc / cnt - u
        mixed = jnp.dot(pooled.astype(BF16), pw_ref[gi], preferred_element_type=F32)
        o_ref[0, :, c0:c0 + POOL_GROUP] = (mixed * ps_ref[:, c0:c0 + POOL_GROUP]).astype(o_ref.dtype)


def _pool(proj3, hist16, pw_bf16, pscale, tp, pos0):
    b, l, _ = proj3.shape
    return pl.pallas_call(
        functools.partial(_pool_kernel, tp=tp, pos0=pos0),
        out_shape=jax.ShapeDtypeStruct((b, l, W_POOL), BF16),
        grid=(b, l // tp),
        in_specs=[
            pl.BlockSpec((1, tp, W_POOL), lambda s, i: (s, i, 0)),
            pl.BlockSpec((1, POOL_HIST + 1, W_POOL), lambda s, i: (s, 0, 0)),
            _const_spec((len(POOL_SIZES), POOL_GROUP, POOL_GROUP)),
            _const_spec((1, W_POOL)),
        ],
        out_specs=pl.BlockSpec((1, tp, W_POOL), lambda s, i: (s, i, 0)),
        scratch_shapes=[pltpu.VMEM((POOL_HIST + 1 + tp, W_POOL), F32)],
        compiler_params=_params("parallel", "arbitrary"),
        name="pool",
    )(proj3, hist16, pw_bf16, pscale)


def _s5_kernel(u_ref, h0r_ref, h0i_ref, bblk_ref, cblk_ref, sc_ref, d_ref, wglu_ref,
               o_ref, hr_out, hi_out, bu_ref, xs_ref, y_ref, cr_ref, ci_ref, *, ts):
    i = pl.program_id(1)

    @pl.when(i == 0)
    def _():
        cr_ref[...] = h0r_ref[0]
        ci_ref[...] = h0i_ref[0]

    u = u_ref[0]
    ub = u.astype(BF16)
    for sb in range(SSM_SB):
        bu_ref[...] = jnp.dot(ub[:, sb * SSM_SB_CH:(sb + 1) * SSM_SB_CH], bblk_ref[sb],
                              preferred_element_type=F32)
        for lc in range(SSM_SB_LANES // SCAN_LANES):
            re = slice(lc * SCAN_LANES, (lc + 1) * SCAN_LANES)
            im = slice(SSM_SB_LANES + lc * SCAN_LANES, SSM_SB_LANES + (lc + 1) * SCAN_LANES)
            a1r, a1i, a2r, a2i, a4r, a4i = [sc_ref[sb, k, :, re] for k in range(6)]
            steps = ((a1r, a1i, 1), (a2r, a2i, 2), (a4r, a4i, 4))

            def local_scan(r, _, re=re, im=im, steps=steps):
                rows = pl.ds(pl.multiple_of(r * SCAN_ROWS, SCAN_ROWS), SCAN_ROWS)
                xr = bu_ref[rows, re]
                xi = bu_ref[rows, im]
                for ar, ai, d in steps:
                    sr = pltpu.roll(xr, d, 0)
                    si = pltpu.roll(xi, d, 0)
                    xr, xi = xr + ar * sr - ai * si, xi + ar * si + ai * sr
                xs_ref[rows, re] = xr
                xs_ref[rows, im] = xi
                return 0

            lax.fori_loop(0, ts // SCAN_ROWS, local_scan, 0, unroll=SCAN_UNROLL)
        for lc in range(SSM_SB_LANES // CARRY_LANES):
            re = slice(lc * CARRY_LANES, (lc + 1) * CARRY_LANES)
            im = slice(SSM_SB_LANES + lc * CARRY_LANES, SSM_SB_LANES + (lc + 1) * CARRY_LANES)
            pr = sc_ref[sb, 6, :, re]
            pi = sc_ref[sb, 7, :, re]
            cr0 = jnp.broadcast_to(cr_ref[sb:sb + 1, re], (SCAN_ROWS, CARRY_LANES))
            ci0 = jnp.broadcast_to(ci_ref[sb:sb + 1, re], (SCAN_ROWS, CARRY_LANES))

            def add_carry(r, carry, re=re, im=im, pr=pr, pi=pi):
                cr, ci = carry
                rows = pl.ds(pl.multiple_of(r * SCAN_ROWS, SCAN_ROWS), SCAN_ROWS)
                hr = xs_ref[rows, re] + pr * cr - pi * ci
                hi = xs_ref[rows, im] + pr * ci + pi * cr
                bu_ref[rows, re] = hr
                bu_ref[rows, im] = hi
                last = slice(SCAN_ROWS - 1, SCAN_ROWS)
                return (jnp.broadcast_to(hr[last], (SCAN_ROWS, CARRY_LANES)),
                        jnp.broadcast_to(hi[last], (SCAN_ROWS, CARRY_LANES)))

            cr, ci = lax.fori_loop(0, ts // SCAN_ROWS, add_carry, (cr0, ci0), unroll=4)
            cr_ref[sb:sb + 1, re] = cr[0:1]
            ci_ref[sb:sb + 1, re] = ci[0:1]
        y_ref[:, sb * SSM_SB_CH:(sb + 1) * SSM_SB_CH] = jnp.dot(
            bu_ref[...].astype(BF16), cblk_ref[sb], preferred_element_type=F32)
    y = jax.nn.gelu(y_ref[...] + d_ref[...] * u)
    z = jnp.dot(y.astype(BF16), wglu_ref[...], preferred_element_type=F32)
    o_ref[0] = (y * jax.nn.sigmoid(z)).astype(o_ref.dtype)

    @pl.when(i == pl.num_programs(1) - 1)
    def _():
        hr_out[0] = cr_ref[...]
        hi_out[0] = ci_ref[...]


def _s5(proj3, h0r, h0i, bblk, cblk, scan_consts, d_skip, wglu_bf16, ts):
    b, l, _ = proj3.shape
    st_spec = pl.BlockSpec((1, SSM_SB, SSM_SB_LANES), lambda s, i: (s, 0, 0))
    st_shape = jax.ShapeDtypeStruct((b, SSM_SB, SSM_SB_LANES), F32)
    return pl.pallas_call(
        functools.partial(_s5_kernel, ts=ts),
        out_shape=(jax.ShapeDtypeStruct((b, l, W_SSM), BF16), st_shape, st_shape),
        grid=(b, l // ts),
        in_specs=[
            pl.BlockSpec((1, ts, W_SSM), lambda s, i: (s, i, 1)),
            st_spec, st_spec,
            _const_spec(bblk.shape), _const_spec(cblk.shape), _const_spec(scan_consts.shape),
            _const_spec((1, W_SSM)), _const_spec((W_SSM, W_SSM)),
        ],
        out_specs=(pl.BlockSpec((1, ts, W_SSM), lambda s, i: (s, i, 0)), st_spec, st_spec),
        scratch_shapes=[
            pltpu.VMEM((ts, 2 * SSM_SB_LANES), F32),
            pltpu.VMEM((ts, 2 * SSM_SB_LANES), F32),
            pltpu.VMEM((ts, W_SSM), F32),
            pltpu.VMEM((SSM_SB, SSM_SB_LANES), F32),
            pltpu.VMEM((SSM_SB, SSM_SB_LANES), F32),
        ],
        compiler_params=_params("parallel", "arbitrary"),
        name="s5",
    )(proj3, h0r, h0i, bblk, cblk, scan_consts, d_skip, wglu_bf16)


def _s5_weights(a_re, a_im, log_step, b_re, b_im, c_re, c_im):
    dt = jnp.exp(log_step)[:, None]
    mag = jnp.exp(a_re * dt)
    abr = mag * jnp.cos(a_im * dt)
    abi = mag * jnp.sin(a_im * dt)
    nr, ni = abr - 1.0, abi
    den = a_re * a_re + a_im * a_im
    f_re = (nr * a_re + ni * a_im) / den
    f_im = (ni * a_re - nr * a_im) / den
    bbr = f_re[..., None] * b_re - f_im[..., None] * b_im
    bbi = f_re[..., None] * b_im + f_im[..., None] * b_re
    gl = N_SSM_GROUPS // SSM_SB
    eye = jnp.eye(gl, dtype=F32)

    def bdiag_in(x):
        x = x.reshape(SSM_SB, gl, SSM_STATE, SSM_GROUP)
        return jnp.einsum("sgpc,gh->sgchp", x, eye).reshape(SSM_SB, gl * SSM_GROUP, gl * SSM_STATE)

    def bdiag_out(x):
        x = x.reshape(SSM_SB, gl, SSM_GROUP, SSM_STATE)
        return jnp.einsum("sgcp,gh->sgphc", x, eye).reshape(SSM_SB, gl * SSM_STATE, gl * SSM_GROUP)

    bblk = jnp.concatenate([bdiag_in(bbr), bdiag_in(bbi)], axis=2).astype(BF16)
    cblk = jnp.concatenate([bdiag_out(c_re), -bdiag_out(c_im)], axis=1).astype(BF16)

    ar = abr.reshape(SSM_SB, SSM_SB_LANES)
    ai = abi.reshape(SSM_SB, SSM_SB_LANES)
    pw_r, pw_i = [ar], [ai]
    for _ in range(SCAN_ROWS - 1):
        pr, pi = pw_r[-1], pw_i[-1]
        pw_r.append(pr * ar - pi * ai)
        pw_i.append(pr * ai + pi * ar)
    row = jnp.arange(SCAN_ROWS)[None, :, None]

    def step(d):
        m = (row >= d).astype(F32)
        return pw_r[d - 1][:, None, :] * m, pw_i[d - 1][:, None, :] * m

    s1r, s1i = step(1)
    s2r, s2i = step(2)
    s4r, s4i = step(4)
    car_r = jnp.stack(pw_r, axis=1)
    car_i = jnp.stack(pw_i, axis=1)
    consts = jnp.stack([s1r, s1i, s2r, s2i, s4r, s4i, car_r, car_i], axis=1)
    return bblk, cblk, consts


def _merge_kernel(x_ref, oa_ref, ob_ref, ga_ref, gb_ref, pa_ref, pb_ref, wo_ref, g2_ref,
                  x2_ref, xnt_ref):
    ma = jnp.dot(oa_ref[0], pa_ref[...], preferred_element_type=F32)
    mb = jnp.dot(ob_ref[0], pb_ref[...], preferred_element_type=F32)
    merged = jax.nn.sigmoid(ga_ref[0]) * ma + jax.nn.sigmoid(gb_ref[0]) * mb
    x2 = x_ref[0] + jnp.dot(merged.astype(BF16), wo_ref[...], preferred_element_type=F32)
    x2_ref[0] = x2
    xnt_ref[...] = _rms(x2, g2_ref[...]).T.astype(BF16)


def _merge(x3, oa3, ob3, proj3, pa, pb, wo, g2, tm, pad_blocks):
    b, l, _ = x3.shape
    off = pad_blocks
    nblk = l // tm
    return pl.pallas_call(
        _merge_kernel,
        out_shape=(jax.ShapeDtypeStruct((b, l, D_MODEL), F32), jax.ShapeDtypeStruct((D_MODEL, b * l), BF16)),
        grid=(b, nblk),
        in_specs=[
            pl.BlockSpec((1, tm, D_MODEL), lambda s, i: (s, i, 0)),
            pl.BlockSpec((1, tm, W_POOL), lambda s, i: (s, i + off, 0)),
            pl.BlockSpec((1, tm, W_SSM), lambda s, i: (s, i + off, 0)),
            pl.BlockSpec((1, tm, D_MODEL), lambda s, i: (s, i + off, 1)),
            pl.BlockSpec((1, tm, D_MODEL), lambda s, i: (s, i + off, 2)),
            _const_spec((W_POOL, D_MODEL)), _const_spec((W_SSM, D_MODEL)),
            _const_spec((D_MODEL, D_MODEL)), _const_spec((1, D_MODEL)),
        ],
        out_specs=(pl.BlockSpec((1, tm, D_MODEL), lambda s, i: (s, i, 0)),
                   pl.BlockSpec((D_MODEL, tm), lambda s, i: (0, s * nblk + i))),
        compiler_params=_params("parallel", "parallel"),
        name="merge",
    )(x3, oa3, ob3, proj3, proj3, pa, pb, wo, g2)


LANE = 128
SUBLANE = 8
PACK_ROWS = 16
PEER_TOKENS = 512
KEY_ROWS = PEER_TOKENS // LANE
SLAB_ROWS = N_KEYS * KEY_ROWS
HEAD_PAIR = SUBLANE // KEY_ROWS
SLAB_PAD = SUBLANE


def _slab_shift(kind, head):
    return kind + 2 * (head % HEAD_PAIR)


def _query_kernel(xnt_ref, wqt_ref, k1_ref, k2_ref, s_ref):
    qt = jnp.dot(wqt_ref[...], xnt_ref[...], preferred_element_type=F32).astype(BF16)
    for h in range(PEER_HEADS):
        r0 = h * D_QUERY
        halves = (jnp.dot(k1_ref[...], qt[r0:r0 + D_HALF], preferred_element_type=F32),
                  jnp.dot(k2_ref[...], qt[r0 + D_HALF:r0 + D_QUERY], preferred_element_type=F32))
        for half, s in enumerate(halves):
            for j in range(KEY_ROWS):
                s_ref[h, half, 0, pl.ds(j, N_KEYS, stride=KEY_ROWS), :] = s[:, j * 128:(j + 1) * 128]


def _query(xnt, wqt, k1, k2):
    n = xnt.shape[1]
    nblk = n // PEER_TOKENS
    return pl.pallas_call(
        _query_kernel,
        out_shape=jax.ShapeDtypeStruct((PEER_HEADS, 2, nblk, SLAB_ROWS, 128), F32),
        grid=(nblk,),
        in_specs=[
            pl.BlockSpec((D_MODEL, PEER_TOKENS), lambda i: (0, i)),
            _const_spec((PEER_HEADS * D_QUERY, D_MODEL)),
            _const_spec((N_KEYS, D_HALF)), _const_spec((N_KEYS, D_HALF)),
        ],
        out_specs=pl.BlockSpec((PEER_HEADS, 2, 1, SLAB_ROWS, 128), lambda i: (0, 0, i, 0, 0)),
        compiler_params=_params("parallel"),
        name="query",
    )(xnt, wqt, k1, k2)


def _oddeven_merge_sort_pairs(n):
    pairs = []
    p = 1
    while p < n:
        k = p
        while k >= 1:
            for j in range(k % p, n - k, 2 * k):
                for i in range(min(k, n - j - k)):
                    if (i + j) // (p * 2) == (i + j + k) // (p * 2):
                        pairs.append((i + j, i + j + k))
            k //= 2
        p *= 2
    return pairs


_SORT16_PAIRS = _oddeven_merge_sort_pairs(PEER_TOPK)


def _cmpx(v, i, j):
    a, b = v[i], v[j]
    v[i] = jnp.maximum(a, b)
    v[j] = jnp.minimum(a, b)


def _sort16(v):
    v = list(v)
    for i, j in _SORT16_PAIRS:
        _cmpx(v, i, j)
    return v


def _merge_top16(a, b):
    k = PEER_TOPK
    c = [a[i] if b[k - 1 - i] is None else jnp.maximum(a[i], b[k - 1 - i]) for i in range(k)]
    d = k // 2
    while d >= 1:
        for i in range(k):
            if not i & d:
                _cmpx(c, i, i + d)
        d //= 2
    return c


def _top16_values(rows):
    groups = [_sort16(rows[g:g + PEER_TOPK]) for g in range(0, len(rows), PEER_TOPK)]
    while len(groups) > 1:
        groups = [_merge_top16(groups[g], groups[g + 1]) for g in range(0, len(groups), 2)]
    return groups[0]


def _top16(val):
    rows = val.shape[0]
    rowid = lax.broadcasted_iota(jnp.int32, val.shape, 0).astype(F32)
    kshape = (PEER_TOPK,) + val.shape[1:]
    kid = lax.broadcasted_iota(jnp.int32, kshape, 0)

    def body(r, carry):
        val, rank, tops = carry
        m = jnp.max(val, axis=0, keepdims=True)
        idx = jnp.min(jnp.where(val == m, rowid, float(rows)), axis=0, keepdims=True)
        hit = rowid == idx
        rank = jnp.where(hit, lax.convert_element_type(r, F32), rank)
        val = jnp.where(hit, -jnp.inf, val)
        tops = jnp.where(kid == r, m, tops)
        return val, rank, tops

    init = (val, jnp.full(val.shape, float(PEER_TOPK), F32), jnp.zeros(kshape, F32))
    _, rank, tops = lax.fori_loop(0, PEER_TOPK, body, init)
    return rank, tops


def _outer16(a, b, op):
    return jnp.concatenate([op(a[r:r + 1], b) for r in range(PEER_TOPK)], axis=0)


def _gates_kernel(s_ref, c1_ref, e1_ref, a2_ref, e2_ref):
    k = PEER_TOPK

    def load_key(half, i):
        rows = slice(i * KEY_ROWS, (i + 1) * KEY_ROWS)
        return jnp.concatenate([s_ref[hh, half, 0, rows, :] for hh in range(HEAD_PAIR)], axis=0)

    def store_key(ref, i, val, kind=None):
        for hh in range(HEAD_PAIR):
            shift = 0 if kind is None else _slab_shift(kind, hh)
            rows = slice(i * KEY_ROWS + shift, (i + 1) * KEY_ROWS + shift)
            ref[hh, 0, rows, :] = val[hh * KEY_ROWS:(hh + 1) * KEY_ROWS]

    for ref in (a2_ref, e2_ref):
        for hh in range(HEAD_PAIR):
            ref[hh, 0, 0:SUBLANE, :] = jnp.zeros((SUBLANE, LANE), F32)
            ref[hh, 0, SLAB_ROWS:SLAB_ROWS + SLAB_PAD, :] = jnp.zeros((SLAB_PAD, LANE), F32)
    s1 = [load_key(0, i) for i in range(N_KEYS)]
    s2 = [load_key(1, i) for i in range(N_KEYS)]
    v1 = _top16_values(s1)
    v2 = _top16_values(s2)
    width = [k // (r + 1) for r in range(k)]
    cell = [[v1[r1] + v2[r2] for r2 in range(width[r1])] for r1 in range(k)]
    top = cell[0]
    for r1 in range(1, k):
        top = _merge_top16(top, cell[r1] + [None] * (k - width[r1]))
    tau = top[k - 1]

    def count(vals, thr, cmp):
        n = jnp.zeros_like(thr)
        for v in vals:
            n = n + cmp(v, thr).astype(F32)
        return n

    x1 = [jnp.exp(v - v1[0]) for v in v1]
    x2 = [jnp.exp(v - v2[0]) for v in v2]
    z = jnp.zeros_like(tau)
    nrow = []
    for r1 in range(k):
        n = jnp.zeros_like(tau)
        for r2 in range(width[r1]):
            hit = cell[r1][r2] >= tau
            n = n + hit.astype(F32)
            z = z + jnp.where(hit, x1[r1] * x2[r2], 0.0)
        nrow.append(n)
    ncell = nrow[0]
    for n in nrow[1:]:
        ncell = ncell + n
    ge = lambda a, b: a >= b
    ambiguous = ((count(s1, v1[k - 1], ge) != float(k)) | (count(s2, v2[k - 1], ge) != float(k))
                 | (ncell != float(k)))
    for r1 in range(1, k):
        ambiguous = ambiguous | (v1[r1] + v2[width[r1]] >= tau)
    col = []
    for j in range(1, k + 1):
        c = jnp.full_like(tau, jnp.inf)
        for r1 in range(k // j):
            c = jnp.minimum(c, jnp.where(nrow[r1] >= float(j), v1[r1], jnp.inf))
        col.append(c)
    inv_z = 1.0 / z
    for i in range(N_KEYS):
        c1 = jnp.full_like(tau, jnp.inf)
        for j in range(k):
            c1 = jnp.minimum(c1, jnp.where(s1[i] >= col[j], v2[j], jnp.inf))
        store_key(c1_ref, i, c1)
        store_key(a2_ref, i, s2[i], 0)
        store_key(e1_ref, i, jnp.where(s1[i] >= v1[k - 1], jnp.exp(s1[i] - v1[0]) * inv_z, 0.0))
        store_key(e2_ref, i, jnp.where(s2[i] >= v2[k - 1], jnp.exp(s2[i] - v2[0]), 0.0), 1)

    @pl.when(jnp.max(ambiguous.astype(F32)) > 0.0)
    def _():
        t1 = jnp.stack(s1, axis=0)
        t2 = jnp.stack(s2, axis=0)
        rank1, w1 = _top16(t1)
        rank2, w2 = _top16(t2)
        crank, _ = _top16(_outer16(w1, w2, jnp.add))
        sel = (crank < float(k)).astype(F32)
        y1 = jnp.exp(w1 - w1[0:1])
        y2 = jnp.exp(w2 - w2[0:1])
        zz = jnp.sum(sel * _outer16(y1, y2, jnp.multiply), axis=0, keepdims=True)
        n1 = jnp.zeros_like(t1)
        for r in range(k):
            cnt = jnp.sum(sel[r * k:(r + 1) * k], axis=0, keepdims=True)
            n1 = jnp.where(rank1 == float(r), cnt, n1)
        g1 = jnp.exp(t1 - w1[0:1]) / zz
        g2 = jnp.exp(t2 - w2[0:1])
        for i in range(N_KEYS):
            store_key(c1_ref, i, 0.5 - n1[i])
            store_key(a2_ref, i, -rank2[i], 0)
            store_key(e1_ref, i, g1[i])
            store_key(e2_ref, i, g2[i], 1)


def _gates(scores):
    nblk = scores.shape[2]
    spec = pl.BlockSpec((HEAD_PAIR, 1, SLAB_ROWS, LANE), lambda i, g: (g, i, 0, 0))
    shp = jax.ShapeDtypeStruct((PEER_HEADS, nblk, SLAB_ROWS, LANE), F32)
    pspec = pl.BlockSpec((HEAD_PAIR, 1, SLAB_ROWS + SLAB_PAD, LANE), lambda i, g: (g, i, 0, 0))
    pshp = jax.ShapeDtypeStruct((PEER_HEADS, nblk, SLAB_ROWS + SLAB_PAD, LANE), F32)
    return pl.pallas_call(
        _gates_kernel,
        out_shape=(shp, shp, pshp, pshp),
        grid=(nblk, PEER_HEADS // HEAD_PAIR),
        in_specs=[pl.BlockSpec((HEAD_PAIR, 2, 1, SLAB_ROWS, LANE), lambda i, g: (g, 0, i, 0, 0))],
        out_specs=(spec, spec, pspec, pspec),
        compiler_params=_params("parallel", "parallel"),
        name="gates",
    )(scores)


def _peer_kernel(xnt_ref, x2_ref, u_ref, vt_ref, vtl_ref, c1_ref, e1_ref, a2_ref, e2_ref, gf_ref,
                 o_ref, acc_ref, w_ref, coef_ref, *, ec, t):
    c = pl.program_id(1)

    @pl.when(c == 0)
    def _():
        coef_ref[...] = jnp.zeros((ec, t), BF16)

    kt = MXU_DIM
    for lt in range(t // LANE):
        lanes = slice(lt * LANE, (lt + 1) * LANE)
        for j in range(ec // N_KEYS):
            krow = slice(j * KEY_ROWS + lt, j * KEY_ROWS + lt + 1)
            c1b = [jnp.broadcast_to(c1_ref[h, 0, krow, :], (SUBLANE, LANE)) for h in range(PEER_HEADS)]
            e1b = [jnp.broadcast_to(e1_ref[h, 0, krow, :], (SUBLANE, LANE)) for h in range(PEER_HEADS)]
            for ib in range(N_KEYS // SUBLANE):
                row0 = ib * SUBLANE * KEY_ROWS + lt
                rows = lambda kind, h: pl.ds(row0 + _slab_shift(kind, h), SUBLANE, stride=KEY_ROWS)
                w = jnp.zeros((SUBLANE, LANE), F32)
                for h in range(PEER_HEADS):
                    hit = a2_ref[h, 0, rows(0, h), :] >= c1b[h]
                    w = w + jnp.where(hit, e1b[h] * e2_ref[h, 0, rows(1, h), :], 0.0)
                w_ref[j * N_KEYS + ib * SUBLANE:j * N_KEYS + (ib + 1) * SUBLANE, lanes] = w
    prev = jnp.where(c == 0, 0.0, acc_ref[...])
    acc_ref[...] = prev + jnp.dot(vt_ref[:, 0:kt], coef_ref[0:kt, :], preferred_element_type=F32)
    for k0 in range(kt, ec, kt):
        acc_ref[...] += jnp.dot(vt_ref[:, k0:k0 + kt], coef_ref[k0:k0 + kt, :], preferred_element_type=F32)
    act = jax.nn.gelu(jnp.dot(u_ref[...], xnt_ref[...], preferred_element_type=F32))
    coef_ref[...] = (w_ref[...] * act).astype(BF16)

    @pl.when(c == pl.num_programs(1) - 1)
    def _():
        total = acc_ref[...]
        for k0 in range(0, ec, kt):
            total = total + jnp.dot(vtl_ref[:, k0:k0 + kt], coef_ref[k0:k0 + kt, :], preferred_element_type=F32)
        o_ref[...] = _rms(x2_ref[...] + total.T, gf_ref[...])


def _peer(xnt, x2, u_bf16, vt_bf16, n1, e1, r2, e2, gf, t, ec):
    n = xnt.shape[1]
    nc = N_EXPERTS // ec
    assert t == PEER_TOKENS
    kc = ec // N_KEYS
    prv = lambda c: jnp.maximum(c - 1, 0)
    cspec = pl.BlockSpec((PEER_HEADS, 1, kc * KEY_ROWS, LANE), lambda b, c: (0, b, c, 0))
    gspec = pl.BlockSpec((PEER_HEADS, 1, SLAB_ROWS + SLAB_PAD, LANE), lambda b, c: (0, b, 0, 0))
    return pl.pallas_call(
        functools.partial(_peer_kernel, ec=ec, t=t),
        out_shape=jax.ShapeDtypeStruct((n, D_MODEL), F32),
        grid=(n // t, nc),
        in_specs=[
            pl.BlockSpec((D_MODEL, t), lambda b, c: (0, b)),
            pl.BlockSpec((t, D_MODEL), lambda b, c: (b, 0)),
            pl.BlockSpec((ec, D_MODEL), lambda b, c: (c, 0)),
            pl.BlockSpec((D_MODEL, ec), lambda b, c: (0, prv(c))),
            pl.BlockSpec((D_MODEL, ec), lambda b, c: (0, nc - 1)),
            cspec, cspec, gspec, gspec,
            _const_spec((1, D_MODEL)),
        ],
        out_specs=pl.BlockSpec((t, D_MODEL), lambda b, c: (b, 0)),
        scratch_shapes=[
            pltpu.VMEM((D_MODEL, t), F32),
            pltpu.VMEM((ec, t), F32),
            pltpu.VMEM((ec, t), BF16),
        ],
        compiler_params=_params("parallel", "arbitrary"),
        name="peer",
    )(xnt, x2, u_bf16, vt_bf16, vt_bf16, n1, e1, r2, e2, gf)


def _group(x_real, front, hist16, h0r, h0i, wts, *, pos0, t_seq):
    b, l, _ = x_real.shape
    if front is None:
        pad_rows = 0
        proj3 = _inproj(x_real.reshape(1, b * l, D_MODEL), None, wts["norm1_g"], wts["w_in"], T_INPROJ)
        proj3 = proj3.reshape(b, l, W_IN_COLS)
    else:
        pad_rows = front.shape[0]
        proj3 = _inproj(x_real, front, wts["norm1_g"], wts["w_in"], T_INPROJ)
    lp = pad_rows + l
    out_a = _pool(proj3, hist16, wts["pool_w"], wts["pool_scale"], t_seq, pos0)
    out_b, hr, hi = _s5(proj3, h0r, h0i, wts["bblk"], wts["cblk"], wts["scan"], wts["ssm_d"], wts["w_glu"], t_seq)
    n = b * l
    if pad_rows == 0:
        x_real, out_a, out_b, projm = (a.reshape(1, n, a.shape[-1]) for a in (x_real, out_a, out_b, proj3))
    else:
        projm = proj3
    x2, xnt = _merge(x_real, out_a, out_b, projm, wts["proj_a"], wts["proj_b"], wts["w_out"],
                     wts["norm2_g"], T_MERGE, pad_rows // T_MERGE)
    scores = _query(xnt, wts["peer_wqt"], wts["peer_key1"], wts["peer_key2"])
    n1, e1, r2, e2 = _gates(scores)
    y = _peer(xnt, x2.reshape(n, D_MODEL), wts["peer_u"], wts["peer_vt"], n1, e1, r2, e2,
              wts["final_norm_g"], PEER_TOKENS, PEER_CHUNK)
    new_hist = proj3[:, lp - POOL_HIST:, :W_POOL]
    shp = (b, N_SSM_GROUPS, SSM_STATE)
    return y.reshape(b, l, D_MODEL), new_hist, hr.reshape(shp), hi.reshape(shp)


def kernel(x_prompt, x_sample, cache_pool, state_ssm_re, state_ssm_im, meta_tokens, norm1_g, w_in, pool_w, pool_scale, ssm_a_re, ssm_a_im, ssm_log_step, ssm_b_re, ssm_b_im, ssm_c_re, ssm_c_im, ssm_d, w_glu, proj_a, proj_b, w_out, norm2_g, peer_wq, peer_key1, peer_key2, peer_u, peer_v, final_norm_g):
    depth = norm1_g.shape[0]
    assert depth == 1
    bp, seq, _ = x_prompt.shape
    bs, dseq, _ = x_sample.shape

    bblk, cblk, scan = _s5_weights(ssm_a_re[0], ssm_a_im[0], ssm_log_step[0], ssm_b_re[0], ssm_b_im[0],
                                   ssm_c_re[0], ssm_c_im[0])
    wts = {
        "norm1_g": norm1_g[0][None], "w_in": w_in[0].astype(BF16),
        "pool_w": pool_w[0].astype(BF16), "pool_scale": pool_scale[0][None],
        "bblk": bblk, "cblk": cblk, "scan": scan, "ssm_d": ssm_d[0][None], "w_glu": w_glu[0].astype(BF16),
        "proj_a": proj_a[0].astype(BF16), "proj_b": proj_b[0].astype(BF16), "w_out": w_out[0].astype(BF16),
        "norm2_g": norm2_g[0][None], "peer_wqt": peer_wq[0].astype(BF16).T,
        "peer_key1": peer_key1[0].astype(BF16), "peer_key2": peer_key2[0].astype(BF16),
        "peer_u": peer_u[0].astype(BF16), "peer_vt": peer_v[0].astype(BF16).T,
        "final_norm_g": final_norm_g[None],
    }

    front = jnp.concatenate([jnp.zeros((T_INPROJ - N_META, D_MODEL), x_prompt.dtype), meta_tokens], axis=0)
    zst = jnp.zeros((bp, SSM_SB, SSM_SB_LANES), F32)
    yp, pool_p, re_p, im_p = _group(
        x_prompt, front, jnp.zeros((bp, POOL_HIST + 1, W_POOL), F32), zst, zst, wts,
        pos0=1 - (T_INPROJ - N_META), t_seq=T_SEQ)

    hist16 = jnp.pad(cache_pool[0], ((0, 0), (1, 0), (0, 0)))
    ys, pool_s, re_s, im_s = _group(
        x_sample, None, hist16,
        state_ssm_re[0].reshape(bs, SSM_SB, SSM_SB_LANES), state_ssm_im[0].reshape(bs, SSM_SB, SSM_SB_LANES),
        wts, pos0=1 + PAST_LEN, t_seq=dseq)

    return (yp, ys, pool_p[None], re_p[None], im_p[None], pool_s[None], re_s[None], im_s[None])
```
